```python
import math
import numpy as np
import jax
import jax.numpy as jnp
from jax import lax

D_MODEL = 1024
BATCH = 8
SEQ = 4096
DEPTH = 4

GRID_W = 64
CTX_LEN = 256
N_MIXERS = 3
N_NA_LAYERS = (DEPTH + 2) // 3
N_HY_LAYERS = (DEPTH + 1) // 3
N_GM_LAYERS = DEPTH // 3

NA_HEADS = 16
NA_HEAD_DIM = D_MODEL // NA_HEADS
WIN_H = 8
WIN_W = 16
NA_QBLK = 16
NA_BAND = 32

HY_EMB = 33
HY_ORDER = 64
HY_SHORT = 3
HY_FAST_DECAY = 0.3
HY_SLOW_DECAY = 1.5
HY_DECAY_TARGET = 1e-2

GM_CHUNK = 128
GM_WIDTH = 3 * D_MODEL
GM_GROUPS = 16

N_EXPERTS = 32
TOP_K = 4
D_FF = D_MODEL
SWIGLU_ALPHA = 1.702
SWIGLU_LIMIT = 7.0
MOE_BLOCK = 128

EPS = 1e-6
NEG_INF = -1e30

kernel_name = 'hybrid_natten_hyena_gmlp_moe_dit'


def rms_norm(x, g):
    xf = x.astype(jnp.float32)
    y = xf * lax.rsqrt(jnp.mean(xf * xf, axis=-1, keepdims=True) + EPS)
    return (y * g.astype(jnp.float32)).astype(x.dtype)


def layer_norm(x, g, b):
    xf = x.astype(jnp.float32)
    mu = jnp.mean(xf, axis=-1, keepdims=True)
    var = jnp.mean(jnp.square(xf - mu), axis=-1, keepdims=True)
    y = (xf - mu) * lax.rsqrt(var + EPS) * g.astype(jnp.float32) + b.astype(jnp.float32)
    return y.astype(x.dtype)


def modulate(h, shift, scale):
    return h * (1 + scale) + shift


def _na_column_tables():
    n_blk = GRID_W // NA_QBLK
    q_cols = np.arange(GRID_W).reshape(n_blk, NA_QBLK)
    win_start = np.clip(q_cols - WIN_W // 2, 0, GRID_W - WIN_W)
    band_start = np.clip(np.arange(n_blk) * NA_QBLK - WIN_W // 2, 0, GRID_W - NA_BAND)
    band_cols = band_start[:, None] + np.arange(NA_BAND)
    kc = band_cols[:, None, :]
    valid = (kc >= win_start[:, :, None]) & (kc < win_start[:, :, None] + WIN_W)
    dcol = np.clip(kc - q_cols[:, :, None] + WIN_W - 1, 0, 2 * WIN_W - 2)
    return band_cols, valid, dcol


def neighbourhood_attention(hx, hc, w_qkv, rpb, w_o, need_ctx):
    B, L, D = hx.shape
    H, Dh = NA_HEADS, NA_HEAD_DIM
    rows = L // GRID_W
    kh = min(WIN_H, rows)
    n_blk = GRID_W // NA_QBLK
    scale = Dh ** -0.5
    band_cols, col_valid, dcol = _na_column_tables()
    col_valid = jnp.asarray(col_valid)[:, :, None, :]

    q, k, v = jnp.split(hx @ w_qkv, 3, axis=-1)
    q = q.reshape(B, rows, GRID_W, H, Dh)
    k = k.reshape(B, rows, GRID_W, H, Dh)
    v = v.reshape(B, rows, GRID_W, H, Dh)
    if need_ctx:
        qc, kc, vc = jnp.split(hc @ w_qkv, 3, axis=-1)
        qc = qc.reshape(B, -1, H, Dh)
    else:
        kc, vc = jnp.split(hc @ w_qkv[:, D:], 2, axis=-1)
    kc = kc.reshape(B, -1, H, Dh)
    vc = vc.reshape(B, -1, H, Dh)
    n_loc = kh * NA_BAND

    def row_block(r):
        r0 = jnp.clip(r - kh // 2, 0, rows - kh)
        k_band = lax.dynamic_slice_in_dim(k, r0, kh, axis=1)[:, :, band_cols]
        v_band = lax.dynamic_slice_in_dim(v, r0, kh, axis=1)[:, :, band_cols]
        q_row = lax.dynamic_index_in_dim(q, r, axis=1, keepdims=False).reshape(B, n_blk, NA_QBLK, H, Dh)
        s_loc = jnp.einsum('bjqhd,bkjchd->bhjqkc', q_row, k_band,
                           preferred_element_type=jnp.float32) * scale
        drow = r0 + jnp.arange(kh) - r + WIN_H - 1
        bias = rpb[:, drow][:, :, dcol].transpose(0, 2, 3, 1, 4)
        s_loc = jnp.where(col_valid, s_loc + bias.astype(jnp.float32), NEG_INF)
        s_ctx = jnp.einsum('bjqhd,bmhd->bhjqm', q_row, kc,
                           preferred_element_type=jnp.float32) * scale
        s_all = jnp.concatenate([s_loc.reshape(B, H, n_blk, NA_QBLK, n_loc), s_ctx], axis=-1)
        p = jax.nn.softmax(s_all, axis=-1).astype(v.dtype)
        p_loc = p[..., :n_loc].reshape(B, H, n_blk, NA_QBLK, kh, NA_BAND)
        p_ctx = p[..., n_loc:]
        o = (jnp.einsum('bhjqkc,bkjchd->bjqhd', p_loc, v_band)
             + jnp.einsum('bhjqm,bmhd->bjqhd', p_ctx, vc))
        return o.reshape(B, GRID_W, D)

    o = lax.map(row_block, jnp.arange(rows))
    y = o.transpose(1, 0, 2, 3).reshape(B, L, D) @ w_o
    if not need_ctx:
        return y, None
    s = jnp.einsum('bqhd,bkhd->bhqk', qc, kc, preferred_element_type=jnp.float32) * scale
    p = jax.nn.softmax(s, axis=-1).astype(vc.dtype)
    yc = jnp.einsum('bhqk,bkhd->bqhd', p, vc).reshape(B, -1, D) @ w_o
    return y, yc


def short_conv(u, w, b):
    y = lax.conv_general_dilated(u, w[:, None, :].astype(u.dtype), window_strides=(1,),
                                 padding=((HY_SHORT // 2, HY_SHORT // 2),),
                                 dimension_numbers=('NWC', 'WIO', 'NWC'),
                                 feature_group_count=u.shape[-1])
    return y + b


def hyena_filter(L, f_w1, f_b1, f_w2, f_b2, f_w3, f_b3, f_freq, f_wout):
    f32 = jnp.float32
    t = jnp.linspace(0.0, 1.0, L, dtype=f32)[:, None]
    bands = (HY_EMB - 1) // 2
    w = 2 * math.pi * jnp.arange(L, dtype=f32)[:, None] / L
    f = jnp.linspace(1e-4, bands - 1, bands, dtype=f32)[None, :]
    z = jnp.concatenate([t, jnp.cos(f * w), -jnp.sin(f * w)], axis=-1)
    freq = f_freq.astype(f32)
    a = jnp.sin(freq * (z @ f_w1.astype(f32) + f_b1.astype(f32)))
    a = jnp.sin(freq * (a @ f_w2.astype(f32) + f_b2.astype(f32)))
    a = jnp.sin(freq * (a @ f_w3.astype(f32) + f_b3.astype(f32)))
    h = a @ f_wout.astype(f32)
    deltas = jnp.linspace(math.log(HY_FAST_DECAY) / HY_DECAY_TARGET,
                          math.log(HY_SLOW_DECAY) / HY_DECAY_TARGET, D_MODEL, dtype=f32)
    h = h * jnp.exp(-t * jnp.abs(jnp.concatenate([deltas, deltas])))
    h_fwd, h_bwd = h[:, :D_MODEL], h[:, D_MODEL:]
    k = jnp.concatenate([h_fwd[:1] + h_bwd[:1], h_fwd[1:], jnp.zeros((1, D_MODEL), f32),
                         h_bwd[:0:-1]], axis=0)
    return k / jnp.sum(jnp.abs(k), axis=0, keepdims=True)


def long_conv(v, k, skip):
    L = v.shape[1]
    vf = jnp.fft.rfft(v.astype(jnp.float32), n=2 * L, axis=1)
    kf = jnp.fft.rfft(k, axis=0)
    y = jnp.fft.irfft(vf * kf[None], n=2 * L, axis=1)[:, :L]
    return (y + v.astype(jnp.float32) * skip.astype(jnp.float32)).astype(v.dtype)


def hyena_mixer(h, w_in, b_in, conv_w, conv_b, f_w1, f_b1, f_w2, f_b2, f_w3, f_b3, f_freq,
                f_wout, skip, w_out, b_out):
    L = h.shape[1]
    u = short_conv(h @ w_in + b_in, conv_w, conv_b)
    x0, x1, v = jnp.split(u, 3, axis=-1)
    k = hyena_filter(L, f_w1, f_b1, f_w2, f_b2, f_w3, f_b3, f_freq, f_wout)
    y = long_conv(v * x1, k, skip) * x0
    return y @ w_out + b_out


def chunk_gmlp(h, w_in, b_in, ln_g, ln_b, w_s, b_s, w_out, b_out):
    B, L, _ = h.shape
    z = jax.nn.gelu(h @ w_in + b_in, approximate=False)
    u, v = jnp.split(z, 2, axis=-1)
    v = layer_norm(v, ln_g, ln_b).reshape(B, L // GM_CHUNK, GM_CHUNK, GM_GROUPS, GM_WIDTH // GM_GROUPS)
    v = jnp.einsum('gpq,bnqgc->bnpgc', w_s, v) + b_s.T[None, None, :, :, None]
    return (u * v.reshape(B, L, GM_WIDTH)) @ w_out + b_out


def moe_ffn(h, router_w, router_b, w_gu, b_gu, w_dn, b_dn):
    T, D = h.shape
    n_assign = T * TOP_K
    logits = jnp.dot(h, router_w, preferred_element_type=jnp.float32) + router_b.astype(jnp.float32)
    top_logit, top_e = lax.top_k(logits, TOP_K)
    gate = jax.nn.softmax(top_logit, axis=-1).reshape(-1)
    flat_e = top_e.reshape(-1)
    flat_tok = jnp.arange(n_assign, dtype=jnp.int32) // TOP_K
    order = jnp.argsort(flat_e)
    e_sorted, tok_sorted, gate_sorted = flat_e[order], flat_tok[order], gate[order]
    counts = jnp.bincount(flat_e, length=N_EXPERTS)
    padded = (counts + MOE_BLOCK - 1) // MOE_BLOCK * MOE_BLOCK
    pad_end = jnp.cumsum(padded)
    pad_start = pad_end - padded
    grp_start = jnp.cumsum(counts) - counts
    dest = pad_start[e_sorted] + jnp.arange(n_assign, dtype=jnp.int32) - grp_start[e_sorted]
    n_blocks = -(-n_assign // MOE_BLOCK) + N_EXPERTS
    buf = jnp.zeros((n_blocks * MOE_BLOCK, D), h.dtype).at[dest].set(h[tok_sorted])
    blk_e = jnp.minimum(jnp.searchsorted(pad_end, jnp.arange(n_blocks, dtype=jnp.int32) * MOE_BLOCK,
                                         side='right'), N_EXPERTS - 1)

    def expert_block(args):
        xb, e = args
        gu = xb @ w_gu[e] + b_gu[e]
        x_glu = jnp.minimum(gu[:, :D_FF], SWIGLU_LIMIT)
        x_lin = jnp.clip(gu[:, D_FF:], -SWIGLU_LIMIT, SWIGLU_LIMIT)
        act = x_glu * jax.nn.sigmoid(SWIGLU_ALPHA * x_glu) * (x_lin + 1)
        return act @ w_dn[e] + b_dn[e]

    out = lax.map(expert_block, (buf.reshape(n_blocks, MOE_BLOCK, D), blk_e)).reshape(-1, D)
    y = out[dest].astype(jnp.float32) * gate_sorted[:, None]
    return jax.ops.segment_sum(y, tok_sorted, num_segments=T).astype(h.dtype)


def setup_inputs(seed: int = 0) -> dict:
    key = jax.random.key(seed)
    keys = iter(jax.random.split(key, 48))
    D = D_MODEL

    def nrm(shape, scale):
        return jax.random.normal(next(keys), shape, jnp.float32) * scale

    def gain(shape):
        return 1.0 + nrm(shape, 0.1)

    return {
        'x': nrm((BATCH, SEQ, D), 1.0),
        'c': nrm((BATCH, D), 1.0),
        'ctx': nrm((BATCH, CTX_LEN, D), 1.0),
        'c_ctx': nrm((D,), 1.0),
        'ada_w': nrm((DEPTH, D, 6 * D), 0.5 * D ** -0.5),
        'ada_b': nrm((DEPTH, 6 * D), 0.01),
        'norm1_g': gain((DEPTH, D)),
        'norm2_g': gain((DEPTH, D)),
        'final_g': gain((D,)),
        'na_w_qkv': nrm((N_NA_LAYERS, D, 3 * D), D ** -0.5),
        'na_rpb': nrm((N_NA_LAYERS, NA_HEADS, 2 * WIN_H - 1, 2 * WIN_W - 1), 0.2),
        'na_w_o': nrm((N_NA_LAYERS, D, D), D ** -0.5),
        'hy_w_in': nrm((N_HY_LAYERS, D, 3 * D), D ** -0.5),
        'hy_b_in': nrm((N_HY_LAYERS, 3 * D), 0.01),
        'hy_conv_w': nrm((N_HY_LAYERS, HY_SHORT, 3 * D), HY_SHORT ** -0.5),
        'hy_conv_b': nrm((N_HY_LAYERS, 3 * D), 0.01),
        'hy_f_w1': nrm((N_HY_LAYERS, HY_EMB, HY_ORDER), HY_EMB ** -0.5),
        'hy_f_b1': nrm((N_HY_LAYERS, HY_ORDER), 0.1),
        'hy_f_w2': nrm((N_HY_LAYERS, HY_ORDER, HY_ORDER), HY_ORDER ** -0.5),
        'hy_f_b2': nrm((N_HY_LAYERS, HY_ORDER), 0.1),
        'hy_f_w3': nrm((N_HY_LAYERS, HY_ORDER, HY_ORDER), HY_ORDER ** -0.5),
        'hy_f_b3': nrm((N_HY_LAYERS, HY_ORDER), 0.1),
        'hy_f_freq': gain((N_HY_LAYERS, HY_ORDER)),
        'hy_f_wout': nrm((N_HY_LAYERS, HY_ORDER, 2 * D), HY_ORDER ** -0.5),
        'hy_skip': nrm((N_HY_LAYERS, D), 1.0),
        'hy_w_out': nrm((N_HY_LAYERS, D, D), D ** -0.5),
        'hy_b_out': nrm((N_HY_LAYERS, D), 0.01),
        'gm_w_in': nrm((N_GM_LAYERS, D, 2 * GM_WIDTH), D ** -0.5),
        'gm_b_in': nrm((N_GM_LAYERS, 2 * GM_WIDTH), 0.01),
        'gm_ln_g': gain((N_GM_LAYERS, GM_WIDTH)),
        'gm_ln_b': nrm((N_GM_LAYERS, GM_WIDTH), 0.01),
        'gm_w_s': nrm((N_GM_LAYERS, GM_GROUPS, GM_CHUNK, GM_CHUNK), GM_CHUNK ** -0.5),
        'gm_b_s': gain((N_GM_LAYERS, GM_GROUPS, GM_CHUNK)),
        'gm_w_out': nrm((N_GM_LAYERS, GM_WIDTH, D), GM_WIDTH ** -0.5),
        'gm_b_out': nrm((N_GM_LAYERS, D), 0.01),
        'moe_router_w': nrm((DEPTH, D, N_EXPERTS), D ** -0.5),
        'moe_router_b': nrm((DEPTH, N_EXPERTS), 0.01),
        'moe_w_gu': nrm((DEPTH, N_EXPERTS, D, 2 * D_FF), D ** -0.5),
        'moe_b_gu': nrm((DEPTH, N_EXPERTS, 2 * D_FF), 0.01),
        'moe_w_dn': nrm((DEPTH, N_EXPERTS, D_FF, D), D_FF ** -0.5),
        'moe_b_dn': nrm((DEPTH, N_EXPERTS, D), 0.01),
    }


def reference(x, c, ctx, c_ctx, ada_w, ada_b, norm1_g, norm2_g, final_g,
              na_w_qkv, na_rpb, na_w_o,
              hy_w_in, hy_b_in, hy_conv_w, hy_conv_b, hy_f_w1, hy_f_b1, hy_f_w2, hy_f_b2,
              hy_f_w3, hy_f_b3, hy_f_freq, hy_f_wout, hy_skip, hy_w_out, hy_b_out,
              gm_w_in, gm_b_in, gm_ln_g, gm_ln_b, gm_w_s, gm_b_s, gm_w_out, gm_b_out,
              moe_router_w, moe_router_b, moe_w_gu, moe_b_gu, moe_w_dn, moe_b_dn):
    B, L, D = x.shape
    n_ctx = ctx.shape[1]
    silu_c = jax.nn.silu(c)
    silu_cc = jax.nn.silu(c_ctx)[None]
    cx = ctx
    for i in range(DEPTH):
        kind, j = i % N_MIXERS, i // N_MIXERS
        ctx_out = i < DEPTH - 1
        mod_x = (silu_c @ ada_w[i] + ada_b[i])[:, None, :]
        mod_c = (silu_cc @ ada_w[i] + ada_b[i])[:, None, :]
        sh1, sc1, g1, sh2, sc2, g2 = jnp.split(mod_x, 6, axis=-1)
        csh1, csc1, cg1, csh2, csc2, cg2 = jnp.split(mod_c, 6, axis=-1)

        hx = modulate(rms_norm(x, norm1_g[i]), sh1, sc1)
        hc = modulate(rms_norm(cx, norm1_g[i]), csh1, csc1) if (ctx_out or kind == 0) else None
        if kind == 0:
            yx, yc = neighbourhood_attention(hx, hc, na_w_qkv[j], na_rpb[j], na_w_o[j], ctx_out)
        elif kind == 1:
            hy = (hy_w_in[j], hy_b_in[j], hy_conv_w[j], hy_conv_b[j], hy_f_w1[j], hy_f_b1[j],
                  hy_f_w2[j], hy_f_b2[j], hy_f_w3[j], hy_f_b3[j], hy_f_freq[j], hy_f_wout[j],
                  hy_skip[j], hy_w_out[j], hy_b_out[j])
            yx = hyena_mixer(hx, *hy)
            yc = hyena_mixer(hc, *hy) if ctx_out else None
        else:
            gm = (gm_w_in[j], gm_b_in[j], gm_ln_g[j], gm_ln_b[j], gm_w_s[j], gm_b_s[j],
                  gm_w_out[j], gm_b_out[j])
            yx = chunk_gmlp(hx, *gm)
            yc = chunk_gmlp(hc, *gm) if ctx_out else None
        x = x + g1 * yx

        moe_p = (moe_router_w[i], moe_router_b[i], moe_w_gu[i], moe_b_gu[i], moe_w_dn[i], moe_b_dn[i])
        hx2 = modulate(rms_norm(x, norm2_g[i]), sh2, sc2).reshape(B * L, D)
        if ctx_out:
            cx = cx + cg1 * yc
            hc2 = modulate(rms_norm(cx, norm2_g[i]), csh2, csc2).reshape(B * n_ctx, D)
            y = moe_ffn(jnp.concatenate([hx2, hc2], axis=0), *moe_p)
            x = x + g2 * y[:B * L].reshape(B, L, D)
            cx = cx + cg2 * y[B * L:].reshape(B, n_ctx, D)
        else:
            x = x + g2 * moe_ffn(hx2, *moe_p).reshape(B, L, D)
    return rms_norm(x, final_g)
```

```python
import functools
import math

import numpy as np
import jax
import jax.numpy as jnp
from jax import lax
from jax.experimental import pallas as pl
from jax.experimental.pallas import tpu as pltpu

F32 = jnp.float32
BF16 = jnp.bfloat16
HIGHEST = lax.Precision.HIGHEST

D = 1024
NB = 8
L = 4096
C = 256
DEPTH = 4
TX = NB * L
TCX = NB * C
T = TX + TCX
EPS = 1e-6

GRID_W = 64
GRID_H = L // GRID_W
NA_HEADS = 16
NA_HD = D // NA_HEADS
WIN_H = 8
WIN_W = 16

HY_EMB = 33
HY_SHORT = 3
HY_FAST_DECAY = 0.3
HY_SLOW_DECAY = 1.5
HY_DECAY_TARGET = 1e-2

GM_CHUNK = 128
GM_WIDTH = 3 * D
GM_GROUPS = 16
GM_GC = GM_WIDTH // GM_GROUPS

N_EXPERTS = 32
TOP_K = 4
D_FF = D
SWIGLU_ALPHA = 1.702
SWIGLU_LIMIT = 7.0

LANES = 128
MOD_ROWS = 16
VMEM_LIMIT = 56 * 1024 * 1024
TM = 512
TM_GM = 256
BM = 512
NEG = -1e30


def _cparams(sem):
    return pltpu.CompilerParams(dimension_semantics=sem, vmem_limit_bytes=VMEM_LIMIT)


def _grp(i, tm):
    return jnp.minimum((i * tm) // L, NB)


def _ada_kernel(c_ref, w_ref, b_ref, o_ref):
    c = c_ref[...]
    s = c * jax.nn.sigmoid(c)
    o_ref[0] = jnp.dot(s, w_ref[0], precision=HIGHEST, preferred_element_type=F32) + b_ref[0]


def ada_modulation(c_all, ada_w, ada_b):
    n6 = ada_w.shape[-1] // D
    return pl.pallas_call(
        _ada_kernel,
        grid=(DEPTH, n6),
        in_specs=[
            pl.BlockSpec((MOD_ROWS, D), lambda l, j: (0, 0)),
            pl.BlockSpec((1, D, D), lambda l, j: (l, 0, j)),
            pl.BlockSpec((1, 1, D), lambda l, j: (l, 0, j)),
        ],
        out_specs=pl.BlockSpec((1, MOD_ROWS, D), lambda l, j: (l, 0, j)),
        out_shape=jax.ShapeDtypeStruct((DEPTH, MOD_ROWS, n6 * D), F32),
        compiler_params=_cparams(("arbitrary", "arbitrary")),
        name="ada_mod",
    )(c_all, ada_w, ada_b.reshape(DEPTH, 1, n6 * D))


def _norm_mod(x, g, sh, sc):
    ms = jnp.mean(x * x, axis=-1, keepdims=True)
    return x * lax.rsqrt(ms + EPS) * g * (1.0 + sc) + sh


def _nmm_kernel(x_ref, g_ref, sh_ref, sc_ref, w_ref, b_ref, o_ref):
    h = _norm_mod(x_ref[...], g_ref[...], sh_ref[0], sc_ref[0])
    acc = jnp.dot(h.astype(BF16), w_ref[...], preferred_element_type=F32) + b_ref[...]
    o_ref[...] = acc.astype(o_ref.dtype)


def _gelu(z):
    return 0.5 * z * (1.0 + lax.erf(z * (1.0 / math.sqrt(2.0))))


def _nmm_gm_kernel(x_ref, g_ref, sh_ref, sc_ref, w_ref, b_ref, lg_ref, lb_ref, o_ref):
    j = pl.program_id(0)
    h = _norm_mod(x_ref[...], g_ref[...], sh_ref[0], sc_ref[0])
    acc = jnp.dot(h.astype(BF16), w_ref[...], preferred_element_type=F32) + b_ref[...]
    z = _gelu(acc)

    @pl.when(j == 0)
    def _():
        o_ref[...] = z.astype(o_ref.dtype)

    @pl.when(j == 1)
    def _():
        mu = jnp.mean(z, axis=-1, keepdims=True)
        zc = z - mu
        var = jnp.mean(zc * zc, axis=-1, keepdims=True)
        o_ref[...] = (zc * lax.rsqrt(var + EPS) * lg_ref[...] + lb_ref[...]).astype(o_ref.dtype)


def norm_mod_matmul(x, g, mod3, k_shift, k_scale, w, b, *, tn, out_dtype, tm=TM, ln=None):
    rows = x.shape[0]
    n = w.shape[1]
    in_specs = [
        pl.BlockSpec((tm, D), lambda j, i: (i, 0)),
        pl.BlockSpec((1, D), lambda j, i: (0, 0)),
        pl.BlockSpec((1, 1, D), lambda j, i: (_grp(i, tm) * 6 + k_shift, 0, 0)),
        pl.BlockSpec((1, 1, D), lambda j, i: (_grp(i, tm) * 6 + k_scale, 0, 0)),
        pl.BlockSpec((D, tn), lambda j, i: (0, j)),
        pl.BlockSpec((1, tn), lambda j, i: (0, j)),
    ]
    args = [x, g.reshape(1, D), mod3, mod3, w, b.reshape(1, n)]
    if ln is None:
        body = _nmm_kernel
    else:
        body = _nmm_gm_kernel
        in_specs += [pl.BlockSpec((1, tn), lambda j, i: (0, 0)), pl.BlockSpec((1, tn), lambda j, i: (0, 0))]
        args += [ln[0].reshape(1, tn), ln[1].reshape(1, tn)]
    return pl.pallas_call(
        body,
        grid=(n // tn, rows // tm),
        in_specs=in_specs,
        out_specs=pl.BlockSpec((tm, tn), lambda j, i: (i, j)),
        out_shape=jax.ShapeDtypeStruct((rows, n), out_dtype),
        compiler_params=_cparams(("arbitrary", "arbitrary")),
        name="norm_mod_matmul",
    )(*args)


def _mm_res_kernel(y_ref, w_ref, b_ref, x_ref, gate_ref, o_ref):
    acc = jnp.dot(y_ref[...].astype(BF16), w_ref[...], preferred_element_type=F32) + b_ref[...]
    o_ref[...] = x_ref[...] + gate_ref[0] * acc


def matmul_gated_residual(y, w, b, x, mod3, k_gate, *, rows, tm=TM):
    k = w.shape[0]
    return pl.pallas_call(
        _mm_res_kernel,
        grid=(rows // tm,),
        in_specs=[
            pl.BlockSpec((tm, k), lambda i: (i, 0)),
            pl.BlockSpec((k, D), lambda i: (0, 0)),
            pl.BlockSpec((1, D), lambda i: (0, 0)),
            pl.BlockSpec((tm, D), lambda i: (i, 0)),
            pl.BlockSpec((1, 1, D), lambda i: (_grp(i, tm) * 6 + k_gate, 0, 0)),
        ],
        out_specs=pl.BlockSpec((tm, D), lambda i: (i, 0)),
        out_shape=jax.ShapeDtypeStruct(x.shape, F32),
        input_output_aliases={3: 0},
        compiler_params=_cparams(("arbitrary",)),
        name="matmul_gated_residual",
    )(y, w, b.reshape(1, D), x, mod3)


def _router_kernel(x_ref, g_ref, sh_ref, sc_ref, wh_ref, wl_ref, rb_ref, h_ref, ro_ref, cnt_ref):
    tm = x_ref.shape[0]
    h = _norm_mod(x_ref[...], g_ref[...], sh_ref[0], sc_ref[0])
    hh = h.astype(BF16)
    hl = (h - hh.astype(F32)).astype(BF16)
    h_ref[...] = hh
    wh = wh_ref[...]
    logits = (jnp.dot(hh, wh, preferred_element_type=F32)
              + jnp.dot(hh, wl_ref[...], preferred_element_type=F32)
              + jnp.dot(hl, wh, preferred_element_type=F32)) + rb_ref[...]
    lane = lax.broadcasted_iota(jnp.int32, (tm, LANES), 1).astype(F32)
    cur = logits
    tops, idxs, hots = [], [], []
    for _ in range(TOP_K):
        m = jnp.max(cur, axis=-1, keepdims=True)
        idx = jnp.min(jnp.where(cur == m, lane, float(LANES)), axis=-1, keepdims=True)
        hot = lane == idx
        tops.append(m)
        idxs.append(idx)
        hots.append(hot)
        cur = jnp.where(hot, -jnp.inf, cur)
    es = [jnp.exp(t - tops[0]) for t in tops]
    den = es[0] + es[1] + es[2] + es[3]
    memb = jnp.zeros((tm, LANES), F32)
    for hot in hots:
        memb = memb + jnp.where(hot, 1.0, 0.0)
    row = lax.broadcasted_iota(jnp.int32, (tm, tm), 0)
    col = lax.broadcasted_iota(jnp.int32, (tm, tm), 1)
    tri = jnp.where(row > col, 1.0, 0.0).astype(BF16)
    pref = jnp.dot(tri, memb.astype(BF16), preferred_element_type=F32)
    ro = jnp.zeros((tm, LANES), F32)
    for k in range(TOP_K):
        rank = jnp.sum(jnp.where(hots[k], pref, 0.0), axis=-1, keepdims=True)
        ro = jnp.where(lane == k, es[k] / den, ro)
        ro = jnp.where(lane == TOP_K + k, idxs[k], ro)
        ro = jnp.where(lane == 2 * TOP_K + k, rank, ro)
    ro_ref[...] = ro
    cnt_ref[0] = jnp.broadcast_to(jnp.sum(memb, axis=0, keepdims=True), (8, LANES))


def moe_router(x, g, mod3, rw_hi, rw_lo, rb_pad, *, rows, tm=TM):
    nt = rows // tm
    return pl.pallas_call(
        _router_kernel,
        grid=(nt,),
        in_specs=[
            pl.BlockSpec((tm, D), lambda i: (i, 0)),
            pl.BlockSpec((1, D), lambda i: (0, 0)),
            pl.BlockSpec((1, 1, D), lambda i: (_grp(i, tm) * 6 + 3, 0, 0)),
            pl.BlockSpec((1, 1, D), lambda i: (_grp(i, tm) * 6 + 4, 0, 0)),
            pl.BlockSpec((D, LANES), lambda i: (0, 0)),
            pl.BlockSpec((D, LANES), lambda i: (0, 0)),
            pl.BlockSpec((1, LANES), lambda i: (0, 0)),
        ],
        out_specs=[
            pl.BlockSpec((tm, D), lambda i: (i, 0)),
            pl.BlockSpec((tm, LANES), lambda i: (i, 0)),
            pl.BlockSpec((1, 8, LANES), lambda i: (i, 0, 0)),
        ],
        out_shape=[
            jax.ShapeDtypeStruct((rows, D), BF16),
            jax.ShapeDtypeStruct((rows, LANES), F32),
            jax.ShapeDtypeStruct((nt, 8, LANES), F32),
        ],
        compiler_params=_cparams(("arbitrary",)),
        name="moe_router",
    )(x, g.reshape(1, D), mod3, mod3, rw_hi, rw_lo, rb_pad)


def _expert_kernel(te_ref, nu_ref, x_ref, wgu_ref, bgu_ref, wdn_ref, bdn_ref, o_ref):
    i = pl.program_id(0)

    @pl.when(i < nu_ref[0])
    def _():
        x = x_ref[...]
        glu = jnp.dot(x, wgu_ref[0, :, :D_FF], preferred_element_type=F32) + bgu_ref[0, :, :D_FF]
        lin = jnp.dot(x, wgu_ref[0, :, D_FF:], preferred_element_type=F32) + bgu_ref[0, :, D_FF:]
        glu = jnp.minimum(glu, SWIGLU_LIMIT)
        lin = jnp.clip(lin, -SWIGLU_LIMIT, SWIGLU_LIMIT)
        act = glu * jax.nn.sigmoid(SWIGLU_ALPHA * glu) * (lin + 1.0)
        o_ref[...] = jnp.dot(act.astype(BF16), wdn_ref[0], preferred_element_type=F32) + bdn_ref[0]

    @pl.when(i >= nu_ref[0])
    def _():
        o_ref[...] = jnp.zeros(o_ref.shape, o_ref.dtype)


def moe_experts(tile_e, n_used, xs, w_gu, b_gu, w_dn, b_dn):
    p = xs.shape[0]
    grid_spec = pltpu.PrefetchScalarGridSpec(
        num_scalar_prefetch=2,
        grid=(p // BM,),
        in_specs=[
            pl.BlockSpec((BM, D), lambda i, te, nu: (i, 0)),
            pl.BlockSpec((1, D, 2 * D_FF), lambda i, te, nu: (te[i], 0, 0)),
            pl.BlockSpec((1, 1, 2 * D_FF), lambda i, te, nu: (te[i], 0, 0)),
            pl.BlockSpec((1, D_FF, D), lambda i, te, nu: (te[i], 0, 0)),
            pl.BlockSpec((1, 1, D), lambda i, te, nu: (te[i], 0, 0)),
        ],
        out_specs=pl.BlockSpec((BM, D), lambda i, te, nu: (i, 0)),
    )
    return pl.pallas_call(
        _expert_kernel,
        grid_spec=grid_spec,
        out_shape=jax.ShapeDtypeStruct((p, D), F32),
        compiler_params=_cparams(("arbitrary",)),
        name="moe_experts",
    )(tile_e, n_used, xs, w_gu, b_gu.reshape(N_EXPERTS, 1, 2 * D_FF), w_dn, b_dn.reshape(N_EXPERTS, 1, D))


def _combine_kernel(y_ref, ro_ref, x_ref, gate_ref, o_ref):
    ro = ro_ref[...]
    acc = y_ref[0] * ro[:, 0:1]
    for k in range(1, TOP_K):
        acc = acc + y_ref[k] * ro[:, k:k + 1]
    o_ref[...] = x_ref[...] + gate_ref[0] * acc


def moe_combine(yk, ro, x, mod3, *, rows, tm=256):
    return pl.pallas_call(
        _combine_kernel,
        grid=(rows // tm,),
        in_specs=[
            pl.BlockSpec((TOP_K, tm, D), lambda i: (0, i, 0)),
            pl.BlockSpec((tm, LANES), lambda i: (i, 0)),
            pl.BlockSpec((tm, D), lambda i: (i, 0)),
            pl.BlockSpec((1, 1, D), lambda i: (_grp(i, tm) * 6 + 5, 0, 0)),
        ],
        out_specs=pl.BlockSpec((tm, D), lambda i: (i, 0)),
        out_shape=jax.ShapeDtypeStruct(x.shape, F32),
        input_output_aliases={2: 0},
        compiler_params=_cparams(("arbitrary",)),
        name="moe_combine",
    )(yk, ro, x, mod3)


def moe_layer(x, mod3, norm2_g, router_w, router_b, w_gu, b_gu, w_dn, b_dn, *, rows):
    rw_hi = router_w.astype(BF16)
    rw_lo = (router_w - rw_hi.astype(F32)).astype(BF16)
    pad = ((0, 0), (0, LANES - N_EXPERTS))
    rw_hi = jnp.pad(rw_hi, pad)
    rw_lo = jnp.pad(rw_lo, pad)
    rb_pad = jnp.pad(router_b.reshape(1, N_EXPERTS), pad, constant_values=NEG)
    h2, ro, cnt = moe_router(x, norm2_g, mod3, rw_hi, rw_lo, rb_pad, rows=rows)

    nt = rows // TM
    n_assign = rows * TOP_K
    n_tiles = n_assign // BM + N_EXPERTS
    ct = cnt[:, 0, :N_EXPERTS].astype(jnp.int32)
    tile_off = jnp.cumsum(ct, axis=0) - ct
    counts = jnp.sum(ct, axis=0)
    padded = (counts + BM - 1) // BM * BM
    pad_end = jnp.cumsum(padded)
    pad_start = pad_end - padded
    base = pad_start[None, :] + tile_off
    idx = ro[:, TOP_K:2 * TOP_K].astype(jnp.int32)
    rank = ro[:, 2 * TOP_K:3 * TOP_K].astype(jnp.int32)
    base_tok = jnp.repeat(base, TM, axis=0)
    pos = jnp.take_along_axis(base_tok, idx, axis=1) + rank
    tok = jnp.repeat(jnp.arange(rows, dtype=jnp.int32), TOP_K)
    src_row = jnp.zeros((n_tiles * BM,), jnp.int32).at[pos.reshape(-1)].set(tok, unique_indices=True)
    tile_e = jnp.minimum(jnp.searchsorted(pad_end, jnp.arange(n_tiles, dtype=jnp.int32) * BM, side='right'),
                         N_EXPERTS - 1).astype(jnp.int32)
    n_used = (pad_end[-1:] // BM).astype(jnp.int32)

    xs = jnp.take(h2, src_row, axis=0)
    out = moe_experts(tile_e, n_used, xs, w_gu.astype(BF16), b_gu, w_dn.astype(BF16), b_dn)
    yk = jnp.take(out, pos.T, axis=0)
    return moe_combine(yk, ro, x, mod3, rows=rows)


NA_QR = 4
NA_QC = 32
NA_KR = NA_QR + WIN_H - 1
NA_KC = 48
NA_NK = NA_KR * NA_KC
NA_RB = GRID_H // NA_QR
NA_CB = GRID_W // NA_QC
NA_RLO_MAX = GRID_H - NA_KR


def _na_bias_tables(rpb):
    tabs = []
    for rb in (0, 1, NA_RB - 1):
        r_lo = min(max(NA_QR * rb - WIN_H // 2, 0), NA_RLO_MAX)
        for cb in range(NA_CB):
            c_lo = cb * (GRID_W - NA_KC)
            qr = NA_QR * rb + np.arange(NA_QR)[:, None]
            qc = NA_QC * cb + np.arange(NA_QC)[None, :]
            qr = np.broadcast_to(qr, (NA_QR, NA_QC)).reshape(-1)
            qc = np.broadcast_to(qc, (NA_QR, NA_QC)).reshape(-1)
            kr = np.broadcast_to(r_lo + np.arange(NA_KR)[:, None], (NA_KR, NA_KC)).reshape(-1)
            kc = np.broadcast_to(c_lo + np.arange(NA_KC)[None, :], (NA_KR, NA_KC)).reshape(-1)
            r0 = np.clip(qr - WIN_H // 2, 0, GRID_H - WIN_H)
            c0 = np.clip(qc - WIN_W // 2, 0, GRID_W - WIN_W)
            valid = ((kr[None, :] >= r0[:, None]) & (kr[None, :] < r0[:, None] + WIN_H)
                     & (kc[None, :] >= c0[:, None]) & (kc[None, :] < c0[:, None] + WIN_W))
            drow = np.clip(kr[None, :] - qr[:, None] + WIN_H - 1, 0, 2 * WIN_H - 2)
            dcol = np.clip(kc[None, :] - qc[:, None] + WIN_W - 1, 0, 2 * WIN_W - 2)
            tabs.append((valid, drow, dcol))
    valid = jnp.asarray(np.stack([t[0] for t in tabs]))
    drow = jnp.asarray(np.stack([t[1] for t in tabs]))
    dcol = jnp.asarray(np.stack([t[2] for t in tabs]))
    bias = rpb[:, drow, dcol]
    return jnp.where(valid[None], bias, NEG).astype(F32)


def _na_softmax_pv(s_parts, v_parts):
    m = None
    for s in s_parts:
        mi = jnp.max(s, axis=-1, keepdims=True)
        m = mi if m is None else jnp.maximum(m, mi)
    den = None
    o = None
    for s, v in zip(s_parts, v_parts):
        p = jnp.exp(s - m)
        di = jnp.sum(p, axis=-1, keepdims=True)
        oi = jnp.dot(p.astype(BF16), v, preferred_element_type=F32)
        den = di if den is None else den + di
        o = oi if o is None else o + oi
    return o / den


def _na_kernel(q_ref, k_ref, v_ref, qc_ref, kc_ref, vc_ref, bias_ref, o_ref, oc_ref, *, need_ctx):
    nq = NA_QR * NA_QC
    lane = lax.broadcasted_iota(jnp.int32, (1, LANES), 1)
    head0 = lane < NA_HD
    kc = kc_ref[...]
    vc = vc_ref[...]
    scale = NA_HD ** -0.5
    nt = (((1,), (1,)), ((), ()))

    def heads_attend(q, k_parts, v_parts, biases):
        outs = []
        for h in range(2):
            hm = head0 if h == 0 else jnp.logical_not(head0)
            qh = jnp.where(hm, q, jnp.zeros_like(q)) * scale
            s_parts = []
            for kp, bp in zip(k_parts, biases):
                s = lax.dot_general(qh, kp, nt, preferred_element_type=F32)
                if bp is not None:
                    s = s + bp(h)
                s_parts.append(s)
            outs.append(_na_softmax_pv(s_parts, v_parts))
        return jnp.where(head0, outs[0], outs[1])

    def row_block(rb, carry):
        r_lo = jnp.clip(NA_QR * rb - WIN_H // 2, 0, NA_RLO_MAX)
        var = jnp.where(rb == 0, 0, jnp.where(rb == NA_RB - 1, 2, 1))
        for cb in range(NA_CB):
            c_lo = cb * (GRID_W - NA_KC)
            q = jnp.concatenate(
                [q_ref[pl.ds(pl.multiple_of((NA_QR * rb + j) * GRID_W + NA_QC * cb, 32), NA_QC), :]
                 for j in range(NA_QR)], axis=0)
            kb = jnp.concatenate(
                [k_ref[pl.ds(pl.multiple_of((r_lo + j) * GRID_W + c_lo, 16), NA_KC), :]
                 for j in range(NA_KR)], axis=0)
            vb = jnp.concatenate(
                [v_ref[pl.ds(pl.multiple_of((r_lo + j) * GRID_W + c_lo, 16), NA_KC), :]
                 for j in range(NA_KR)], axis=0)
            vi = var * NA_CB + cb
            o2 = heads_attend(q, [kb, kc], [vb, vc], [lambda h: bias_ref[h, vi], None])
            o2 = o2.astype(o_ref.dtype)
            for j in range(NA_QR):
                o_ref[pl.ds(pl.multiple_of((NA_QR * rb + j) * GRID_W + NA_QC * cb, 32), NA_QC), :] = (
                    o2[j * NA_QC:(j + 1) * NA_QC])
        return carry

    lax.fori_loop(0, NA_RB, row_block, 0)

    if need_ctx:
        oc_ref[...] = heads_attend(qc_ref[...], [kc], [vc], [None]).astype(oc_ref.dtype)
    else:
        oc_ref[...] = jnp.zeros(oc_ref.shape, oc_ref.dtype)


def neighbourhood_attention(qkv, bias, *, need_ctx):
    hp_n = NA_HEADS // 2
    ctx0 = TX // C
    kern = functools.partial(_na_kernel, need_ctx=need_ctx)
    return pl.pallas_call(
        kern,
        grid=(hp_n, NB),
        in_specs=[
            pl.BlockSpec((L, LANES), lambda hp, b: (b, hp)),
            pl.BlockSpec((L, LANES), lambda hp, b: (b, hp_n + hp)),
            pl.BlockSpec((L, LANES), lambda hp, b: (b, 2 * hp_n + hp)),
            pl.BlockSpec((C, LANES), lambda hp, b: (ctx0 + b, hp)),
            pl.BlockSpec((C, LANES), lambda hp, b: (ctx0 + b, hp_n + hp)),
            pl.BlockSpec((C, LANES), lambda hp, b: (ctx0 + b, 2 * hp_n + hp)),
            pl.BlockSpec((2, 3 * NA_CB, NA_QR * NA_QC, NA_NK), lambda hp, b: (hp, 0, 0, 0)),
        ],
        out_specs=[
            pl.BlockSpec((L, LANES), lambda hp, b: (b, hp)),
            pl.BlockSpec((C, LANES), lambda hp, b: (b, hp)),
        ],
        out_shape=[
            jax.ShapeDtypeStruct((TX, D), BF16),
            jax.ShapeDtypeStruct((TCX, D), BF16),
        ],
        compiler_params=_cparams(("arbitrary", "arbitrary")),
        name="neighbourhood_attention",
    )(qkv, qkv, qkv, qkv, qkv, qkv, bias)


N2 = 128


def _short_conv_kernel(z0_ref, z1_ref, z2_ref, w0_ref, w1_ref, w2_ref, b0_ref, b1_ref, b2_ref, w_ref, x0_ref):
    ls = z0_ref.shape[0]
    row = lax.broadcasted_iota(jnp.int32, (ls, 1), 0)

    def conv(z_ref, cw_ref, cb_ref):
        z = z_ref[...]
        prev = jnp.where(row == 0, 0.0, pltpu.roll(z, 1, axis=0))
        nxt = jnp.where(row == ls - 1, 0.0, pltpu.roll(z, ls - 1, axis=0))
        cw = cw_ref[...]
        return prev * cw[0:1] + z * cw[1:2] + nxt * cw[2:3] + cb_ref[...]

    x0_ref[...] = conv(z0_ref, w0_ref, b0_ref)
    w_ref[...] = conv(z2_ref, w2_ref, b2_ref) * conv(z1_ref, w1_ref, b1_ref)


def hyena_short_conv(z, conv_w, conv_b, *, seq_len, n_seq, row_block0):
    nd = D // LANES
    zspec = lambda o: pl.BlockSpec((seq_len, LANES), lambda s, j: (row_block0 + s, o * nd + j))
    wspec = lambda o: pl.BlockSpec((HY_SHORT, LANES), lambda s, j: (0, o * nd + j))
    bspec = lambda o: pl.BlockSpec((1, LANES), lambda s, j: (0, o * nd + j))
    ospec = pl.BlockSpec((seq_len, LANES), lambda s, j: (s, j))
    cb = conv_b.reshape(1, 3 * D)
    return pl.pallas_call(
        _short_conv_kernel,
        grid=(n_seq, nd),
        in_specs=[zspec(0), zspec(1), zspec(2), wspec(0), wspec(1), wspec(2), bspec(0), bspec(1), bspec(2)],
        out_specs=[ospec, ospec],
        out_shape=[jax.ShapeDtypeStruct((n_seq * seq_len, D), F32)] * 2,
        compiler_params=_cparams(("arbitrary", "arbitrary")),
        name="hyena_short_conv",
    )(z, z, z, conv_w, conv_w, conv_w, cb, cb, cb)


def _dft_tables(n1h_data):
    n1 = 2 * n1h_data
    n = n1 * N2
    k1 = np.arange(n1)[None, :, None]
    n2 = np.arange(N2)[:, None, None]

    def stage1(n1h):
        tt = N2 * np.arange(n1h)[None, None, :] + n2
        ang = 2.0 * np.pi * ((k1 * tt) % n) / n
        return np.concatenate([np.cos(ang), -np.sin(ang)], axis=1)

    gf_data = stage1(n1h_data)
    gf_full = stage1(n1)
    tt = N2 * np.arange(n1h_data)[None, :, None] + n2
    ang = 2.0 * np.pi * ((tt * np.arange(n1)[None, None, :]) % n) / n
    gi = np.concatenate([np.cos(ang), -np.sin(ang)], axis=2) / n
    a2 = 2.0 * np.pi * ((np.arange(N2)[:, None] * np.arange(N2)[None, :]) % N2) / N2
    c2, s2 = np.cos(a2), np.sin(a2)
    m2 = np.block([[c2, s2], [-s2, c2]])
    return (gf_data.astype(np.float32), gf_full.astype(np.float32), gi.astype(np.float32),
            m2.astype(np.float32))


def _dft1_kernel(x_ref, g_ref, o_ref, *, precise):
    n1 = o_ref.shape[1]
    for j in range(8):
        xj = x_ref[0, :, j, :]
        if precise:
            p = jnp.dot(g_ref[j], xj, precision=HIGHEST, preferred_element_type=F32)
        else:
            p = jnp.dot(g_ref[j], xj.astype(BF16), preferred_element_type=F32)
        o_ref[0, :, 0, j, :] = p[:n1]
        o_ref[0, :, 1, j, :] = p[n1:]


def dft_stage1(x4, g, *, precise):
    s, n1h = x4.shape[0], x4.shape[1]
    n1 = g.shape[1] // 2
    return pl.pallas_call(
        functools.partial(_dft1_kernel, precise=precise),
        grid=(s, N2 // 8),
        in_specs=[
            pl.BlockSpec((1, n1h, 8, D), lambda i, t: (i, 0, t, 0)),
            pl.BlockSpec((8, 2 * n1, n1h), lambda i, t: (t, 0, 0)),
        ],
        out_specs=pl.BlockSpec((1, n1, 2, 8, D), lambda i, t: (i, 0, 0, t, 0)),
        out_shape=jax.ShapeDtypeStruct((s, n1, 2, N2, D), F32),
        compiler_params=_cparams(("arbitrary", "arbitrary")),
        name="dft_stage1",
    )(x4, g if precise else g.astype(BF16))


def _filter_stage2_kernel(a_ref, m_ref, o_ref):
    o_ref[0] = jnp.dot(m_ref[...], a_ref[0, 0], precision=HIGHEST, preferred_element_type=F32)


def filter_stage2(a, m2):
    n1 = a.shape[1]
    return pl.pallas_call(
        _filter_stage2_kernel,
        grid=(n1,),
        in_specs=[
            pl.BlockSpec((1, 1, 2 * N2, D), lambda k: (0, k, 0, 0)),
            pl.BlockSpec((2 * N2, 2 * N2), lambda k: (0, 0)),
        ],
        out_specs=pl.BlockSpec((1, 2 * N2, D), lambda k: (k, 0, 0)),
        out_shape=jax.ShapeDtypeStruct((n1, 2 * N2, D), F32),
        compiler_params=_cparams(("arbitrary",)),
        name="filter_stage2",
    )(a, m2)


def _spectral_kernel(a_ref, kf_ref, m_ref, mt_ref, o_ref):
    b = jnp.dot(m_ref[...], a_ref[0, 0].astype(BF16), preferred_element_type=F32)
    br, bi = b[:N2], b[N2:]
    kr, ki = kf_ref[0, :N2], kf_ref[0, N2:]
    y = jnp.concatenate([br * kr - bi * ki, br * ki + bi * kr], axis=0)
    o_ref[0, 0] = jnp.dot(mt_ref[...], y.astype(BF16), preferred_element_type=F32)


def spectral_multiply(a, kf, m2):
    s, n1 = a.shape[0], a.shape[1]
    m2b = jnp.asarray(m2).astype(BF16)
    return pl.pallas_call(
        _spectral_kernel,
        grid=(n1, s),
        in_specs=[
            pl.BlockSpec((1, 1, 2 * N2, D), lambda k, i: (i, k, 0, 0)),
            pl.BlockSpec((1, 2 * N2, D), lambda k, i: (k, 0, 0)),
            pl.BlockSpec((2 * N2, 2 * N2), lambda k, i: (0, 0)),
            pl.BlockSpec((2 * N2, 2 * N2), lambda k, i: (0, 0)),
        ],
        out_specs=pl.BlockSpec((1, 1, 2 * N2, D), lambda k, i: (i, k, 0, 0)),
        out_shape=jax.ShapeDtypeStruct(a.shape, F32),
        compiler_params=_cparams(("arbitrary", "arbitrary")),
        name="spectral_multiply",
    )(a, kf, m2b, m2b.T)


def _idft1_kernel(z_ref, g_ref, w_ref, x0_ref, skip_ref, o_ref):
    skip = skip_ref[...]
    for j in range(8):
        zj = jnp.concatenate([z_ref[0, :, 0, j, :], z_ref[0, :, 1, j, :]], axis=0).astype(BF16)
        y = jnp.dot(g_ref[j], zj, preferred_element_type=F32)
        o_ref[0, :, j, :] = (y + w_ref[0, :, j, :] * skip) * x0_ref[0, :, j, :]


def idft_stage1_gate(z5, gi, w4, x04, skip):
    s, n1 = z5.shape[0], z5.shape[1]
    n1h = w4.shape[1]
    xspec = pl.BlockSpec((1, n1h, 8, D), lambda i, t: (i, 0, t, 0))
    return pl.pallas_call(
        _idft1_kernel,
        grid=(s, N2 // 8),
        in_specs=[
            pl.BlockSpec((1, n1, 2, 8, D), lambda i, t: (i, 0, 0, t, 0)),
            pl.BlockSpec((8, n1h, 2 * n1), lambda i, t: (t, 0, 0)),
            xspec, xspec,
            pl.BlockSpec((1, D), lambda i, t: (0, 0)),
        ],
        out_specs=xspec,
        out_shape=jax.ShapeDtypeStruct(w4.shape, F32),
        compiler_params=_cparams(("arbitrary", "arbitrary")),
        name="idft_stage1_gate",
    )(z5, jnp.asarray(gi).astype(BF16), w4, x04, skip.reshape(1, D))


def long_conv_gate_two_stage(w, x0, k, skip, *, n_seq, seq_len):
    n1h = seq_len // N2
    n1 = 2 * n1h
    gf_data, gf_full, gi, m2 = _dft_tables(n1h)
    ka = dft_stage1(k.reshape(1, n1, N2, D), jnp.asarray(gf_full), precise=True)
    kf = filter_stage2(ka.reshape(1, n1, 2 * N2, D), jnp.asarray(m2))
    w4 = w.reshape(n_seq, n1h, N2, D)
    a = dft_stage1(w4, jnp.asarray(gf_data), precise=False)
    zf = spectral_multiply(a.reshape(n_seq, n1, 2 * N2, D), kf, m2)
    y = idft_stage1_gate(zf.reshape(n_seq, n1, 2, N2, D), gi, w4, x0.reshape(n_seq, n1h, N2, D), skip)
    return y.reshape(n_seq * seq_len, D)


def _filter_dft_kernel(k_ref, f_ref, o_ref):
    o_ref[...] = jnp.dot(f_ref[...], k_ref[...], precision=HIGHEST, preferred_element_type=F32)


def _direct_conv_kernel(w_ref, x0_ref, kf_ref, ff_ref, fi_ref, skip_ref, o_ref):
    n = kf_ref.shape[0] // 2
    w = w_ref[...]
    b = jnp.dot(ff_ref[...], w.astype(BF16), preferred_element_type=F32)
    br, bi = b[:n], b[n:]
    kr, ki = kf_ref[:n], kf_ref[n:]
    y = jnp.concatenate([br * kr - bi * ki, br * ki + bi * kr], axis=0)
    conv = jnp.dot(fi_ref[...], y.astype(BF16), preferred_element_type=F32)
    o_ref[...] = (conv + w * skip_ref[...]) * x0_ref[...]


def long_conv_gate_direct(w, x0, k, skip, *, n_seq, seq_len):
    n = 2 * seq_len
    ang = 2.0 * np.pi * ((np.arange(n)[:, None] * np.arange(n)[None, :]) % n) / n
    f_full = np.concatenate([np.cos(ang), -np.sin(ang)], axis=0).astype(np.float32)
    f_inv = (np.concatenate([np.cos(ang), -np.sin(ang)], axis=1)[:seq_len] / n).astype(np.float32)
    kf = pl.pallas_call(
        _filter_dft_kernel,
        out_shape=jax.ShapeDtypeStruct((2 * n, D), F32),
        compiler_params=pltpu.CompilerParams(vmem_limit_bytes=VMEM_LIMIT),
        name="filter_dft_direct",
    )(k, jnp.asarray(f_full))
    full = lambda shape: pl.BlockSpec(shape, lambda i: (0, 0))
    return pl.pallas_call(
        _direct_conv_kernel,
        grid=(n_seq,),
        in_specs=[
            pl.BlockSpec((seq_len, D), lambda i: (i, 0)),
            pl.BlockSpec((seq_len, D), lambda i: (i, 0)),
            full((2 * n, D)), full((2 * n, seq_len)), full((seq_len, 2 * n)), full((1, D)),
        ],
        out_specs=pl.BlockSpec((seq_len, D), lambda i: (i, 0)),
        out_shape=jax.ShapeDtypeStruct((n_seq * seq_len, D), F32),
        compiler_params=_cparams(("arbitrary",)),
        name="direct_conv_gate",
    )(w, x0, kf, jnp.asarray(f_full[:, :seq_len]).astype(BF16), jnp.asarray(f_inv).astype(BF16),
      skip.reshape(1, D))


def hyena_filter(seq_len, f_w1, f_b1, f_w2, f_b2, f_w3, f_b3, f_freq, f_wout):
    t = jnp.linspace(0.0, 1.0, seq_len, dtype=F32)[:, None]
    bands = (HY_EMB - 1) // 2
    w = 2 * math.pi * jnp.arange(seq_len, dtype=F32)[:, None] / seq_len
    f = jnp.linspace(1e-4, bands - 1, bands, dtype=F32)[None, :]
    z = jnp.concatenate([t, jnp.cos(f * w), -jnp.sin(f * w)], axis=-1)
    a = jnp.sin(f_freq * (jnp.dot(z, f_w1, precision=HIGHEST) + f_b1))
    a = jnp.sin(f_freq * (jnp.dot(a, f_w2, precision=HIGHEST) + f_b2))
    a = jnp.sin(f_freq * (jnp.dot(a, f_w3, precision=HIGHEST) + f_b3))
    h = jnp.dot(a, f_wout, precision=HIGHEST)
    deltas = jnp.linspace(math.log(HY_FAST_DECAY) / HY_DECAY_TARGET,
                          math.log(HY_SLOW_DECAY) / HY_DECAY_TARGET, D, dtype=F32)
    h = h * jnp.exp(-t * jnp.abs(jnp.concatenate([deltas, deltas])))
    h_fwd, h_bwd = h[:, :D], h[:, D:]
    k = jnp.concatenate([h_fwd[:1] + h_bwd[:1], h_fwd[1:], jnp.zeros((1, D), F32), h_bwd[:0:-1]], axis=0)
    return k / jnp.sum(jnp.abs(k), axis=0, keepdims=True)


def hyena_mixer(x, mod3, norm1_g, p, *, with_ctx):
    z = norm_mod_matmul(x, norm1_g, mod3, 0, 1, p['w_in'].astype(BF16), p['b_in'], tn=D, out_dtype=F32)
    fargs = (p['f_w1'], p['f_b1'], p['f_w2'], p['f_b2'], p['f_w3'], p['f_b3'], p['f_freq'], p['f_wout'])
    w, x0 = hyena_short_conv(z, p['conv_w'], p['conv_b'], seq_len=L, n_seq=NB, row_block0=0)
    y = long_conv_gate_two_stage(w, x0, hyena_filter(L, *fargs), p['skip'], n_seq=NB, seq_len=L)
    if not with_ctx:
        return y
    wc, x0c = hyena_short_conv(z, p['conv_w'], p['conv_b'], seq_len=C, n_seq=NB, row_block0=TX // C)
    yc = long_conv_gate_direct(wc, x0c, hyena_filter(C, *fargs), p['skip'], n_seq=NB, seq_len=C)
    return jnp.concatenate([y, yc], axis=0)


GM_PAIR = 2 * GM_GC


def _gm_gate_kernel(u_ref, v_ref, ws_ref, bias_ref, o_ref):
    tm = u_ref.shape[0]
    lane = lax.broadcasted_iota(jnp.int32, (1, GM_PAIR), 1)
    first = lane < GM_GC
    for ch in range(tm // GM_CHUNK):
        rows = slice(ch * GM_CHUNK, (ch + 1) * GM_CHUNK)
        for gp in range(GM_GROUPS // 2):
            cols = slice(gp * GM_PAIR, (gp + 1) * GM_PAIR)
            vp = v_ref[rows, cols]
            r0 = jnp.dot(ws_ref[2 * gp], vp, preferred_element_type=F32)
            r1 = jnp.dot(ws_ref[2 * gp + 1], vp, preferred_element_type=F32)
            r = jnp.where(first, r0, r1) + bias_ref[:, cols]
            o_ref[rows, cols] = (u_ref[rows, cols].astype(F32) * r).astype(o_ref.dtype)


def gm_spatial_gate(zu, w_s, b_s, *, rows, tm=TM):
    bias = jnp.repeat(b_s.T, GM_GC, axis=1)
    return pl.pallas_call(
        _gm_gate_kernel,
        grid=(rows // tm,),
        in_specs=[
            pl.BlockSpec((tm, GM_WIDTH), lambda i: (i, 0)),
            pl.BlockSpec((tm, GM_WIDTH), lambda i: (i, 1)),
            pl.BlockSpec((GM_GROUPS, GM_CHUNK, GM_CHUNK), lambda i: (0, 0, 0)),
            pl.BlockSpec((GM_CHUNK, GM_WIDTH), lambda i: (0, 0)),
        ],
        out_specs=pl.BlockSpec((tm, GM_WIDTH), lambda i: (i, 0)),
        out_shape=jax.ShapeDtypeStruct((zu.shape[0], GM_WIDTH), BF16),
        compiler_params=_cparams(("arbitrary",)),
        name="gm_spatial_gate",
    )(zu, zu, w_s.astype(BF16), bias)


def _final_norm_kernel(x_ref, g_ref, o_ref):
    x = x_ref[...]
    ms = jnp.mean(x * x, axis=-1, keepdims=True)
    o_ref[...] = x * lax.rsqrt(ms + EPS) * g_ref[...]


def final_norm(x, g, *, rows, tm=TM):
    return pl.pallas_call(
        _final_norm_kernel,
        grid=(rows // tm,),
        in_specs=[pl.BlockSpec((tm, D), lambda i: (i, 0)), pl.BlockSpec((1, D), lambda i: (0, 0))],
        out_specs=pl.BlockSpec((tm, D), lambda i: (i, 0)),
        out_shape=jax.ShapeDtypeStruct((rows, D), F32),
        compiler_params=_cparams(("arbitrary",)),
        name="final_norm",
    )(x, g.reshape(1, D))


def kernel(x, c, ctx, c_ctx, ada_w, ada_b, norm1_g, norm2_g, final_g, na_w_qkv, na_rpb, na_w_o, hy_w_in, hy_b_in, hy_conv_w, hy_conv_b, hy_f_w1, hy_f_b1, hy_f_w2, hy_f_b2, hy_f_w3, hy_f_b3, hy_f_freq, hy_f_wout, hy_skip, hy_w_out, hy_b_out, gm_w_in, gm_b_in, gm_ln_g, gm_ln_b, gm_w_s, gm_b_s, gm_w_out, gm_b_out, moe_router_w, moe_router_b, moe_w_gu, moe_b_gu, moe_w_dn, moe_b_dn):
    assert x.shape == (NB, L, D) and ctx.shape == (NB, C, D)
    xs = jnp.concatenate([x.reshape(TX, D), ctx.reshape(TCX, D)], axis=0)
    c_all = jnp.concatenate([c, c_ctx[None], jnp.zeros((MOD_ROWS - NB - 1, D), F32)], axis=0)
    mod = ada_modulation(c_all, ada_w, ada_b).reshape(DEPTH, MOD_ROWS * 6, 1, D)
    zero_b = jnp.zeros((D,), F32)

    for i in range(DEPTH):
        kind, j = i % 3, i // 3
        ctx_out = i < DEPTH - 1
        rows = T if ctx_out else TX
        mod3 = mod[i]
        if kind == 0:
            qkv = norm_mod_matmul(xs, norm1_g[i], mod3, 0, 1, na_w_qkv[j].astype(BF16), jnp.zeros((3 * D,), F32),
                                  tn=D, out_dtype=BF16)
            o, oc = neighbourhood_attention(qkv, _na_bias_tables(na_rpb[j]), need_ctx=ctx_out)
            y = jnp.concatenate([o, oc], axis=0) if ctx_out else o
            xs = matmul_gated_residual(y, na_w_o[j].astype(BF16), zero_b, xs, mod3, 2, rows=rows)
        elif kind == 1:
            p = dict(w_in=hy_w_in[j], b_in=hy_b_in[j], conv_w=hy_conv_w[j], conv_b=hy_conv_b[j],
                     f_w1=hy_f_w1[j], f_b1=hy_f_b1[j], f_w2=hy_f_w2[j], f_b2=hy_f_b2[j], f_w3=hy_f_w3[j],
                     f_b3=hy_f_b3[j], f_freq=hy_f_freq[j], f_wout=hy_f_wout[j], skip=hy_skip[j])
            y = hyena_mixer(xs, mod3, norm1_g[i], p, with_ctx=ctx_out)
            xs = matmul_gated_residual(y, hy_w_out[j].astype(BF16), hy_b_out[j], xs, mod3, 2, rows=rows)
        else:
            zu = norm_mod_matmul(xs, norm1_g[i], mod3, 0, 1, gm_w_in[j].astype(BF16), gm_b_in[j],
                                 tn=GM_WIDTH, out_dtype=BF16, tm=TM_GM, ln=(gm_ln_g[j], gm_ln_b[j]))
            y = gm_spatial_gate(zu, gm_w_s[j], gm_b_s[j], rows=rows)
            xs = matmul_gated_residual(y, gm_w_out[j].astype(BF16), gm_b_out[j], xs, mod3, 2, rows=rows)
        xs = moe_layer(xs, mod3, norm2_g[i], moe_router_w[i], moe_router_b[i], moe_w_gu[i], moe_b_gu[i],
                       moe_w_dn[i], moe_b_dn[i], rows=rows)
    return final_norm(xs, final_g, rows=TX).reshape(NB, L, D)
```

```python
import functools
import math

import numpy as np
import jax
import jax.numpy as jnp
from jax import lax
from jax.experimental import pallas as pl
from jax.experimental.pallas import tpu as pltpu

F32 = jnp.float32
BF16 = jnp.bfloat16
HIGHEST = lax.Precision.HIGHEST

D = 1024
NB = 8
L = 4096
C = 256
DEPTH = 4
TX = NB * L
TCX = NB * C
T = TX + TCX
EPS = 1e-6

GRID_W = 64
GRID_H = L // GRID_W
NA_HEADS = 16
NA_HD = D // NA_HEADS
WIN_H = 8
WIN_W = 16

HY_EMB = 33
HY_SHORT = 3
HY_FAST_DECAY = 0.3
HY_SLOW_DECAY = 1.5
HY_DECAY_TARGET = 1e-2

GM_CHUNK = 128
GM_WIDTH = 3 * D
GM_GROUPS = 16
GM_GC = GM_WIDTH // GM_GROUPS

N_EXPERTS = 32
TOP_K = 4
D_FF = D
SWIGLU_ALPHA = 1.702
SWIGLU_LIMIT = 7.0

LANES = 128
MOD_ROWS = 16
VMEM_LIMIT = 56 * 1024 * 1024
TM = 512
TM_GM = 256
BM = 512
NEG = -1e30


def _cparams(sem):
    return pltpu.CompilerParams(dimension_semantics=sem, vmem_limit_bytes=VMEM_LIMIT)


def _grp(i, tm):
    return jnp.minimum((i * tm) // L, NB)


def _ada_kernel(c_ref, w_ref, b_ref, o_ref):
    c = c_ref[...]
    s = c * jax.nn.sigmoid(c)
    o_ref[0] = jnp.dot(s, w_ref[0], precision=HIGHEST, preferred_element_type=F32) + b_ref[0]


def ada_modulation(c_all, ada_w, ada_b):
    n6 = ada_w.shape[-1] // D
    return pl.pallas_call(
        _ada_kernel,
        grid=(DEPTH, n6),
        in_specs=[
            pl.BlockSpec((MOD_ROWS, D), lambda l, j: (0, 0)),
            pl.BlockSpec((1, D, D), lambda l, j: (l, 0, j)),
            pl.BlockSpec((1, 1, D), lambda l, j: (l, 0, j)),
        ],
        out_specs=pl.BlockSpec((1, MOD_ROWS, D), lambda l, j: (l, 0, j)),
        out_shape=jax.ShapeDtypeStruct((DEPTH, MOD_ROWS, n6 * D), F32),
        compiler_params=_cparams(("arbitrary", "arbitrary")),
        name="ada_mod",
    )(c_all, ada_w, ada_b.reshape(DEPTH, 1, n6 * D))


def _norm_mod(x, g, sh, sc):
    ms = jnp.mean(x * x, axis=-1, keepdims=True)
    return x * lax.rsqrt(ms + EPS) * g * (1.0 + sc) + sh


def _nmm_kernel(x_ref, g_ref, sh_ref, sc_ref, w_ref, b_ref, o_ref):
    h = _norm_mod(x_ref[...], g_ref[...], sh_ref[0], sc_ref[0])
    acc = jnp.dot(h.astype(BF16), w_ref[...], preferred_element_type=F32) + b_ref[...]
    o_ref[...] = acc.astype(o_ref.dtype)


def _gelu(z):
    return 0.5 * z * (1.0 + lax.erf(z * (1.0 / math.sqrt(2.0))))


def _nmm_gm_kernel(x_ref, g_ref, sh_ref, sc_ref, w_ref, b_ref, lg_ref, lb_ref, o_ref):
    j = pl.program_id(0)
    h = _norm_mod(x_ref[...], g_ref[...], sh_ref[0], sc_ref[0])
    acc = jnp.dot(h.astype(BF16), w_ref[...], preferred_element_type=F32) + b_ref[...]
    z = _gelu(acc)

    @pl.when(j == 0)
    def _():
        o_ref[...] = z.astype(o_ref.dtype)

    @pl.when(j == 1)
    def _():
        mu = jnp.mean(z, axis=-1, keepdims=True)
        zc = z - mu
        var = jnp.mean(zc * zc, axis=-1, keepdims=True)
        o_ref[...] = (zc * lax.rsqrt(var + EPS) * lg_ref[...] + lb_ref[...]).astype(o_ref.dtype)


def norm_mod_matmul(x, g, mod3, k_shift, k_scale, w, b, *, tn, out_dtype, tm=TM, ln=None):
    rows = x.shape[0]
    n = w.shape[1]
    in_specs = [
        pl.BlockSpec((tm, D), lambda j, i: (i, 0)),
        pl.BlockSpec((1, D), lambda j, i: (0, 0)),
        pl.BlockSpec((1, 1, D), lambda j, i: (_grp(i, tm) * 6 + k_shift, 0, 0)),
        pl.BlockSpec((1, 1, D), lambda j, i: (_grp(i, tm) * 6 + k_scale, 0, 0)),
        pl.BlockSpec((D, tn), lambda j, i: (0, j)),
        pl.BlockSpec((1, tn), lambda j, i: (0, j)),
    ]
    args = [x, g.reshape(1, D), mod3, mod3, w, b.reshape(1, n)]
    if ln is None:
        body = _nmm_kernel
    else:
        body = _nmm_gm_kernel
        in_specs += [pl.BlockSpec((1, tn), lambda j, i: (0, 0)), pl.BlockSpec((1, tn), lambda j, i: (0, 0))]
        args += [ln[0].reshape(1, tn), ln[1].reshape(1, tn)]
    return pl.pallas_call(
        body,
        grid=(n // tn, rows // tm),
        in_specs=in_specs,
        out_specs=pl.BlockSpec((tm, tn), lambda j, i: (i, j)),
        out_shape=jax.ShapeDtypeStruct((rows, n), out_dtype),
        compiler_params=_cparams(("arbitrary", "arbitrary")),
        name="norm_mod_matmul",
    )(*args)


def _mm_res_kernel(y_ref, w_ref, b_ref, x_ref, gate_ref, o_ref):
    acc = jnp.dot(y_ref[...].astype(BF16), w_ref[...], preferred_element_type=F32) + b_ref[...]
    o_ref[...] = x_ref[...] + gate_ref[0] * acc


def matmul_gated_residual(y, w, b, x, mod3, k_gate, *, rows, tm=TM):
    k = w.shape[0]
    return pl.pallas_call(
        _mm_res_kernel,
        grid=(rows // tm,),
        in_specs=[
            pl.BlockSpec((tm, k), lambda i: (i, 0)),
            pl.BlockSpec((k, D), lambda i: (0, 0)),
            pl.BlockSpec((1, D), lambda i: (0, 0)),
            pl.BlockSpec((tm, D), lambda i: (i, 0)),
            pl.BlockSpec((1, 1, D), lambda i: (_grp(i, tm) * 6 + k_gate, 0, 0)),
        ],
        out_specs=pl.BlockSpec((tm, D), lambda i: (i, 0)),
        out_shape=jax.ShapeDtypeStruct(x.shape, F32),
        input_output_aliases={3: 0},
        compiler_params=_cparams(("arbitrary",)),
        name="matmul_gated_residual",
    )(y, w, b.reshape(1, D), x, mod3)


def _router_kernel(x_ref, g_ref, sh_ref, sc_ref, wh_ref, wl_ref, rb_ref, h_ref, ro_ref, cnt_ref, run_ref):
    tm = x_ref.shape[0]

    @pl.when(pl.program_id(0) == 0)
    def _():
        run_ref[...] = jnp.zeros(run_ref.shape, F32)

    h = _norm_mod(x_ref[...], g_ref[...], sh_ref[0], sc_ref[0])
    hh = h.astype(BF16)
    hl = (h - hh.astype(F32)).astype(BF16)
    h_ref[...] = hh
    wh = wh_ref[...]
    logits = (jnp.dot(hh, wh, preferred_element_type=F32)
              + jnp.dot(hh, wl_ref[...], preferred_element_type=F32)
              + jnp.dot(hl, wh, preferred_element_type=F32)) + rb_ref[...]
    lane = lax.broadcasted_iota(jnp.int32, (tm, LANES), 1).astype(F32)
    cur = logits
    tops, idxs, hots = [], [], []
    for _ in range(TOP_K):
        m = jnp.max(cur, axis=-1, keepdims=True)
        idx = jnp.min(jnp.where(cur == m, lane, float(LANES)), axis=-1, keepdims=True)
        hot = lane == idx
        tops.append(m)
        idxs.append(idx)
        hots.append(hot)
        cur = jnp.where(hot, -jnp.inf, cur)
    es = [jnp.exp(t - tops[0]) for t in tops]
    den = es[0] + es[1] + es[2] + es[3]
    memb = jnp.zeros((tm, LANES), F32)
    for hot in hots:
        memb = memb + jnp.where(hot, 1.0, 0.0)
    row = lax.broadcasted_iota(jnp.int32, (tm, tm), 0)
    col = lax.broadcasted_iota(jnp.int32, (tm, tm), 1)
    tri = jnp.where(row > col, 1.0, 0.0).astype(BF16)
    run = run_ref[0:1, :]
    pref = jnp.dot(tri, memb.astype(BF16), preferred_element_type=F32) + run
    ro = jnp.zeros((tm, LANES), F32)
    for k in range(TOP_K):
        rank = jnp.sum(jnp.where(hots[k], pref, 0.0), axis=-1, keepdims=True)
        ro = jnp.where(lane == k, es[k] / den, ro)
        ro = jnp.where(lane == TOP_K + k, idxs[k], ro)
        ro = jnp.where(lane == 2 * TOP_K + k, rank, ro)
    ro_ref[...] = ro
    total = jnp.broadcast_to(run + jnp.sum(memb, axis=0, keepdims=True), (8, LANES))
    run_ref[...] = total
    cnt_ref[...] = total


def moe_router(x, g, mod3, rw_hi, rw_lo, rb_pad, *, rows, tm=TM):
    nt = rows // tm
    return pl.pallas_call(
        _router_kernel,
        grid=(nt,),
        in_specs=[
            pl.BlockSpec((tm, D), lambda i: (i, 0)),
            pl.BlockSpec((1, D), lambda i: (0, 0)),
            pl.BlockSpec((1, 1, D), lambda i: (_grp(i, tm) * 6 + 3, 0, 0)),
            pl.BlockSpec((1, 1, D), lambda i: (_grp(i, tm) * 6 + 4, 0, 0)),
            pl.BlockSpec((D, LANES), lambda i: (0, 0)),
            pl.BlockSpec((D, LANES), lambda i: (0, 0)),
            pl.BlockSpec((1, LANES), lambda i: (0, 0)),
        ],
        out_specs=[
            pl.BlockSpec((tm, D), lambda i: (i, 0)),
            pl.BlockSpec((tm, LANES), lambda i: (i, 0)),
            pl.BlockSpec((8, LANES), lambda i: (0, 0)),
        ],
        out_shape=[
            jax.ShapeDtypeStruct((rows, D), BF16),
            jax.ShapeDtypeStruct((rows, LANES), F32),
            jax.ShapeDtypeStruct((8, LANES), F32),
        ],
        scratch_shapes=[pltpu.VMEM((8, LANES), F32)],
        compiler_params=_cparams(("arbitrary",)),
        name="moe_router",
    )(x, g.reshape(1, D), mod3, mod3, rw_hi, rw_lo, rb_pad)


def _slot_kernel(ro_ref, ps_ref, o_ref):
    ro = ro_ref[...]
    lane = lax.broadcasted_iota(jnp.int32, ro.shape, 1).astype(F32)
    ps = ps_ref[...]
    out = ro
    for k in range(TOP_K):
        hot = lane == ro[:, TOP_K + k:TOP_K + k + 1]
        start = jnp.sum(jnp.where(hot, ps, 0.0), axis=-1, keepdims=True)
        out = jnp.where(lane == 2 * TOP_K + k, start + ro[:, 2 * TOP_K + k:2 * TOP_K + k + 1], out)
    o_ref[...] = out


def moe_slots(ro, pad_start, *, rows, tm=TM):
    return pl.pallas_call(
        _slot_kernel,
        grid=(rows // tm,),
        in_specs=[pl.BlockSpec((tm, LANES), lambda i: (i, 0)), pl.BlockSpec((1, LANES), lambda i: (0, 0))],
        out_specs=pl.BlockSpec((tm, LANES), lambda i: (i, 0)),
        out_shape=jax.ShapeDtypeStruct((rows, LANES), F32),
        compiler_params=_cparams(("arbitrary",)),
        name="moe_slots",
    )(ro, pad_start)


def _expert_kernel(te_ref, new_ref, nu_ref, x_ref, wgu_ref, bgu_ref, wdn_ref, bdn_ref, o_ref, wgu_s, wdn_s):
    i = pl.program_id(0)

    @pl.when(new_ref[i] == 1)
    def _():
        wgu_s[...] = wgu_ref[0].astype(BF16)
        wdn_s[...] = wdn_ref[0].astype(BF16)

    @pl.when(i < nu_ref[0])
    def _():
        x = x_ref[...]
        glu = jnp.dot(x, wgu_s[:, :D_FF], preferred_element_type=F32) + bgu_ref[0, :, :D_FF]
        lin = jnp.dot(x, wgu_s[:, D_FF:], preferred_element_type=F32) + bgu_ref[0, :, D_FF:]
        glu = jnp.minimum(glu, SWIGLU_LIMIT)
        lin = jnp.clip(lin, -SWIGLU_LIMIT, SWIGLU_LIMIT)
        act = glu * jax.nn.sigmoid(SWIGLU_ALPHA * glu) * (lin + 1.0)
        out = jnp.dot(act.astype(BF16), wdn_s[...], preferred_element_type=F32) + bdn_ref[0]
        o_ref[...] = out.astype(o_ref.dtype)

    @pl.when(i >= nu_ref[0])
    def _():
        o_ref[...] = jnp.zeros(o_ref.shape, o_ref.dtype)


def moe_experts(tile_e, tile_new, n_used, xs, w_gu, b_gu, w_dn, b_dn):
    p = xs.shape[0]
    grid_spec = pltpu.PrefetchScalarGridSpec(
        num_scalar_prefetch=3,
        grid=(p // BM,),
        in_specs=[
            pl.BlockSpec((BM, D), lambda i, te, tn, nu: (i, 0)),
            pl.BlockSpec((1, D, 2 * D_FF), lambda i, te, tn, nu: (te[i], 0, 0)),
            pl.BlockSpec((1, 1, 2 * D_FF), lambda i, te, tn, nu: (te[i], 0, 0)),
            pl.BlockSpec((1, D_FF, D), lambda i, te, tn, nu: (te[i], 0, 0)),
            pl.BlockSpec((1, 1, D), lambda i, te, tn, nu: (te[i], 0, 0)),
        ],
        out_specs=pl.BlockSpec((BM, D), lambda i, te, tn, nu: (i, 0)),
        scratch_shapes=[pltpu.VMEM((D, 2 * D_FF), BF16), pltpu.VMEM((D_FF, D), BF16)],
    )
    return pl.pallas_call(
        _expert_kernel,
        grid_spec=grid_spec,
        out_shape=jax.ShapeDtypeStruct((p, D), BF16),
        compiler_params=_cparams(("arbitrary",)),
        name="moe_experts",
    )(tile_e, tile_new, n_used, xs, w_gu, b_gu.reshape(N_EXPERTS, 1, 2 * D_FF), w_dn,
      b_dn.reshape(N_EXPERTS, 1, D))


def _combine_kernel(y_ref, ro_ref, x_ref, gate_ref, o_ref):
    ro = ro_ref[...]
    acc = y_ref[0].astype(F32) * ro[:, 0:1]
    for k in range(1, TOP_K):
        acc = acc + y_ref[k].astype(F32) * ro[:, k:k + 1]
    o_ref[...] = x_ref[...] + gate_ref[0] * acc


def moe_combine(yk, ro, x, mod3, *, rows, tm=256):
    return pl.pallas_call(
        _combine_kernel,
        grid=(rows // tm,),
        in_specs=[
            pl.BlockSpec((TOP_K, tm, D), lambda i: (0, i, 0)),
            pl.BlockSpec((tm, LANES), lambda i: (i, 0)),
            pl.BlockSpec((tm, D), lambda i: (i, 0)),
            pl.BlockSpec((1, 1, D), lambda i: (_grp(i, tm) * 6 + 5, 0, 0)),
        ],
        out_specs=pl.BlockSpec((tm, D), lambda i: (i, 0)),
        out_shape=jax.ShapeDtypeStruct(x.shape, F32),
        input_output_aliases={2: 0},
        compiler_params=_cparams(("arbitrary",)),
        name="moe_combine",
    )(yk, ro, x, mod3)


def moe_layer(x, mod3, norm2_g, router_w, router_b, w_gu, b_gu, w_dn, b_dn, *, rows):
    rw_hi = router_w.astype(BF16)
    rw_lo = (router_w - rw_hi.astype(F32)).astype(BF16)
    pad = ((0, 0), (0, LANES - N_EXPERTS))
    rw_hi = jnp.pad(rw_hi, pad)
    rw_lo = jnp.pad(rw_lo, pad)
    rb_pad = jnp.pad(router_b.reshape(1, N_EXPERTS), pad, constant_values=NEG)
    h2, ro, cnt = moe_router(x, norm2_g, mod3, rw_hi, rw_lo, rb_pad, rows=rows)

    n_tiles = rows * TOP_K // BM + N_EXPERTS
    counts = cnt[0, :N_EXPERTS].astype(jnp.int32)
    padded = (counts + BM - 1) // BM * BM
    pad_end = jnp.cumsum(padded)
    pad_start = pad_end - padded
    tile_start = jnp.arange(n_tiles, dtype=jnp.int32) * BM
    tile_e = jnp.minimum(jnp.sum((pad_end[None, :] <= tile_start[:, None]).astype(jnp.int32), axis=1),
                         N_EXPERTS - 1)
    tile_new = jnp.concatenate([jnp.ones((1,), jnp.int32), (tile_e[1:] != tile_e[:-1]).astype(jnp.int32)])
    n_used = pad_end[-1:] // BM

    ps_row = jnp.pad(pad_start.astype(F32).reshape(1, N_EXPERTS), ((0, 0), (0, LANES - N_EXPERTS)))
    ro = moe_slots(ro, ps_row, rows=rows)
    pos = ro[:, 2 * TOP_K:3 * TOP_K].astype(jnp.int32)
    tok = jnp.repeat(jnp.arange(rows, dtype=jnp.int32), TOP_K)
    src_row = jnp.zeros((n_tiles * BM,), jnp.int32).at[pos.reshape(-1)].set(tok, unique_indices=True)

    xs = jnp.take(h2, src_row, axis=0)
    out = moe_experts(tile_e, tile_new, n_used, xs, w_gu, b_gu, w_dn, b_dn)
    yk = jnp.take(out, pos.T, axis=0)
    return moe_combine(yk, ro, x, mod3, rows=rows)


NA_QR = 4
NA_QC = 32
NA_KR = NA_QR + WIN_H - 1
NA_KC = 48
NA_NK = NA_KR * NA_KC
NA_RB = GRID_H // NA_QR
NA_CB = GRID_W // NA_QC
NA_RLO_MAX = GRID_H - NA_KR


def _na_bias_tables(rpb):
    n_dc = 2 * WIN_W - 1
    n_dr = 2 * WIN_H - 1
    sel = np.zeros((n_dc, NA_CB, NA_QC, NA_KC), np.float32)
    for cb in range(NA_CB):
        c_lo = cb * (GRID_W - NA_KC)
        qc = NA_QC * cb + np.arange(NA_QC)[:, None]
        kc = c_lo + np.arange(NA_KC)[None, :]
        c0 = np.clip(qc - WIN_W // 2, 0, GRID_W - WIN_W)
        ok = (kc >= c0) & (kc < c0 + WIN_W)
        dc = kc - qc + WIN_W - 1
        for a, b in zip(*np.nonzero(ok)):
            sel[dc[a, b], cb, a, b] = 1.0
    col_ok = jnp.asarray(sel.sum(axis=0) > 0)
    slabs = jnp.dot(rpb.reshape(NA_HEADS * n_dr, n_dc), jnp.asarray(sel.reshape(n_dc, -1)), precision=HIGHEST)
    slabs = jnp.where(col_ok[None, None], slabs.reshape(NA_HEADS, n_dr, NA_CB, NA_QC, NA_KC), NEG)
    dead = jnp.full((NA_HEADS, NA_QC, NA_KC), NEG, F32)
    tiles = []
    for rb in (0, 1, NA_RB - 1):
        r_lo = min(max(NA_QR * rb - WIN_H // 2, 0), NA_RLO_MAX)
        for cb in range(NA_CB):
            rows = []
            for a in range(NA_QR):
                qr = NA_QR * rb + a
                r0 = min(max(qr - WIN_H // 2, 0), GRID_H - WIN_H)
                row = []
                for j in range(NA_KR):
                    kr = r_lo + j
                    row.append(slabs[:, kr - qr + WIN_H - 1, cb] if r0 <= kr < r0 + WIN_H else dead)
                rows.append(jnp.concatenate(row, axis=-1))
            tiles.append(jnp.concatenate(rows, axis=-2))
    return jnp.stack(tiles, axis=1)


def _na_softmax_pv(s_parts, v_parts):
    m = None
    for s in s_parts:
        mi = jnp.max(s, axis=-1, keepdims=True)
        m = mi if m is None else jnp.maximum(m, mi)
    den = None
    o = None
    for s, v in zip(s_parts, v_parts):
        p = jnp.exp(s - m)
        di = jnp.sum(p, axis=-1, keepdims=True)
        oi = jnp.dot(p.astype(BF16), v, preferred_element_type=F32)
        den = di if den is None else den + di
        o = oi if o is None else o + oi
    return o / den


def _na_kernel(q_ref, k_ref, v_ref, qc_ref, kc_ref, vc_ref, bias_ref, o_ref, oc_ref, *, need_ctx):
    nq = NA_QR * NA_QC
    lane = lax.broadcasted_iota(jnp.int32, (1, LANES), 1)
    head0 = lane < NA_HD
    kc = kc_ref[...]
    vc = vc_ref[...]
    scale = NA_HD ** -0.5
    nt = (((1,), (1,)), ((), ()))

    def heads_attend(q, k_parts, v_parts, biases):
        outs = []
        for h in range(2):
            hm = head0 if h == 0 else jnp.logical_not(head0)
            qh = jnp.where(hm, q, jnp.zeros_like(q)) * scale
            s_parts = []
            for kp, bp in zip(k_parts, biases):
                s = lax.dot_general(qh, kp, nt, preferred_element_type=F32)
                if bp is not None:
                    s = s + bp(h)
                s_parts.append(s)
            outs.append(_na_softmax_pv(s_parts, v_parts))
        return jnp.where(head0, outs[0], outs[1])

    def row_block(rb, carry):
        r_lo = jnp.clip(NA_QR * rb - WIN_H // 2, 0, NA_RLO_MAX)
        var = jnp.where(rb == 0, 0, jnp.where(rb == NA_RB - 1, 2, 1))
        for cb in range(NA_CB):
            c_lo = cb * (GRID_W - NA_KC)
            q = jnp.concatenate(
                [q_ref[pl.ds(pl.multiple_of((NA_QR * rb + j) * GRID_W + NA_QC * cb, 32), NA_QC), :]
                 for j in range(NA_QR)], axis=0)
            kb = jnp.concatenate(
                [k_ref[pl.ds(pl.multiple_of((r_lo + j) * GRID_W + c_lo, 16), NA_KC), :]
                 for j in range(NA_KR)], axis=0)
            vb = jnp.concatenate(
                [v_ref[pl.ds(pl.multiple_of((r_lo + j) * GRID_W + c_lo, 16), NA_KC), :]
                 for j in range(NA_KR)], axis=0)
            vi = var * NA_CB + cb
            o2 = heads_attend(q, [kb, kc], [vb, vc], [lambda h: bias_ref[h, vi], None])
            o2 = o2.astype(o_ref.dtype)
            for j in range(NA_QR):
                o_ref[pl.ds(pl.multiple_of((NA_QR * rb + j) * GRID_W + NA_QC * cb, 32), NA_QC), :] = (
                    o2[j * NA_QC:(j + 1) * NA_QC])
        return carry

    lax.fori_loop(0, NA_RB, row_block, 0, unroll=2)

    if need_ctx:
        oc_ref[...] = heads_attend(qc_ref[...], [kc], [vc], [None]).astype(oc_ref.dtype)
    else:
        oc_ref[...] = jnp.zeros(oc_ref.shape, oc_ref.dtype)


def neighbourhood_attention(qkv, bias, *, need_ctx):
    hp_n = NA_HEADS // 2
    ctx0 = TX // C
    kern = functools.partial(_na_kernel, need_ctx=need_ctx)
    return pl.pallas_call(
        kern,
        grid=(hp_n, NB),
        in_specs=[
            pl.BlockSpec((L, LANES), lambda hp, b: (b, hp)),
            pl.BlockSpec((L, LANES), lambda hp, b: (b, hp_n + hp)),
            pl.BlockSpec((L, LANES), lambda hp, b: (b, 2 * hp_n + hp)),
            pl.BlockSpec((C, LANES), lambda hp, b: (ctx0 + b, hp)),
            pl.BlockSpec((C, LANES), lambda hp, b: (ctx0 + b, hp_n + hp)),
            pl.BlockSpec((C, LANES), lambda hp, b: (ctx0 + b, 2 * hp_n + hp)),
            pl.BlockSpec((2, 3 * NA_CB, NA_QR * NA_QC, NA_NK), lambda hp, b: (hp, 0, 0, 0)),
        ],
        out_specs=[
            pl.BlockSpec((L, LANES), lambda hp, b: (b, hp)),
            pl.BlockSpec((C, LANES), lambda hp, b: (b, hp)),
        ],
        out_shape=[
            jax.ShapeDtypeStruct((TX, D), BF16),
            jax.ShapeDtypeStruct((TCX, D), BF16),
        ],
        compiler_params=_cparams(("arbitrary", "arbitrary")),
        name="neighbourhood_attention",
    )(qkv, qkv, qkv, qkv, qkv, qkv, bias)


N2 = 128


def _short_conv_kernel(z0_ref, z1_ref, z2_ref, w0_ref, w1_ref, w2_ref, b0_ref, b1_ref, b2_ref, w_ref, x0_ref):
    ls = z0_ref.shape[0]
    row = lax.broadcasted_iota(jnp.int32, (ls, 1), 0)

    def conv(z_ref, cw_ref, cb_ref):
        z = z_ref[...]
        prev = jnp.where(row == 0, 0.0, pltpu.roll(z, 1, axis=0))
        nxt = jnp.where(row == ls - 1, 0.0, pltpu.roll(z, ls - 1, axis=0))
        cw = cw_ref[...]
        return prev * cw[0:1] + z * cw[1:2] + nxt * cw[2:3] + cb_ref[...]

    x0_ref[...] = conv(z0_ref, w0_ref, b0_ref)
    w_ref[...] = conv(z2_ref, w2_ref, b2_ref) * conv(z1_ref, w1_ref, b1_ref)


def hyena_short_conv(z, conv_w, conv_b, *, seq_len, n_seq, row_block0):
    nd = D // LANES
    zspec = lambda o: pl.BlockSpec((seq_len, LANES), lambda s, j: (row_block0 + s, o * nd + j))
    wspec = lambda o: pl.BlockSpec((HY_SHORT, LANES), lambda s, j: (0, o * nd + j))
    bspec = lambda o: pl.BlockSpec((1, LANES), lambda s, j: (0, o * nd + j))
    ospec = pl.BlockSpec((seq_len, LANES), lambda s, j: (s, j))
    cb = conv_b.reshape(1, 3 * D)
    return pl.pallas_call(
        _short_conv_kernel,
        grid=(n_seq, nd),
        in_specs=[zspec(0), zspec(1), zspec(2), wspec(0), wspec(1), wspec(2), bspec(0), bspec(1), bspec(2)],
        out_specs=[ospec, ospec],
        out_shape=[jax.ShapeDtypeStruct((n_seq * seq_len, D), F32)] * 2,
        compiler_params=_cparams(("arbitrary", "arbitrary")),
        name="hyena_short_conv",
    )(z, z, z, conv_w, conv_w, conv_w, cb, cb, cb)


def _dft_tables(n1h_data):
    n1 = 2 * n1h_data
    n = n1 * N2
    k1 = np.arange(n1)[None, :, None]
    n2 = np.arange(N2)[:, None, None]

    def stage1(n1h):
        tt = N2 * np.arange(n1h)[None, None, :] + n2
        ang = 2.0 * np.pi * ((k1 * tt) % n) / n
        return np.concatenate([np.cos(ang), -np.sin(ang)], axis=1)

    gf_data = stage1(n1h_data)
    gf_full = stage1(n1)
    tt = N2 * np.arange(n1h_data)[None, :, None] + n2
    ang = 2.0 * np.pi * ((tt * np.arange(n1)[None, None, :]) % n) / n
    gi = np.concatenate([np.cos(ang), -np.sin(ang)], axis=2) / n
    a2 = 2.0 * np.pi * ((np.arange(N2)[:, None] * np.arange(N2)[None, :]) % N2) / N2
    c2, s2 = np.cos(a2), np.sin(a2)
    m2 = np.block([[c2, s2], [-s2, c2]])
    return (gf_data.astype(np.float32), gf_full.astype(np.float32), gi.astype(np.float32),
            m2.astype(np.float32))


def _dft1_kernel(x_ref, g_ref, o_ref, *, precise):
    n1 = o_ref.shape[1]
    for j in range(8):
        xj = x_ref[0, :, j, :]
        if precise:
            p = jnp.dot(g_ref[j], xj, precision=HIGHEST, preferred_element_type=F32)
        else:
            p = jnp.dot(g_ref[j], xj.astype(BF16), preferred_element_type=F32)
        o_ref[0, :, 0, j, :] = p[:n1]
        o_ref[0, :, 1, j, :] = p[n1:]


def dft_stage1(x4, g, *, precise):
    s, n1h = x4.shape[0], x4.shape[1]
    n1 = g.shape[1] // 2
    return pl.pallas_call(
        functools.partial(_dft1_kernel, precise=precise),
        grid=(s, N2 // 8),
        in_specs=[
            pl.BlockSpec((1, n1h, 8, D), lambda i, t: (i, 0, t, 0)),
            pl.BlockSpec((8, 2 * n1, n1h), lambda i, t: (t, 0, 0)),
        ],
        out_specs=pl.BlockSpec((1, n1, 2, 8, D), lambda i, t: (i, 0, 0, t, 0)),
        out_shape=jax.ShapeDtypeStruct((s, n1, 2, N2, D), F32),
        compiler_params=_cparams(("arbitrary", "arbitrary")),
        name="dft_stage1",
    )(x4, g if precise else g.astype(BF16))


def _filter_stage2_kernel(a_ref, m_ref, o_ref):
    o_ref[0] = jnp.dot(m_ref[...], a_ref[0, 0], precision=HIGHEST, preferred_element_type=F32)


def filter_stage2(a, m2):
    n1 = a.shape[1]
    return pl.pallas_call(
        _filter_stage2_kernel,
        grid=(n1,),
        in_specs=[
            pl.BlockSpec((1, 1, 2 * N2, D), lambda k: (0, k, 0, 0)),
            pl.BlockSpec((2 * N2, 2 * N2), lambda k: (0, 0)),
        ],
        out_specs=pl.BlockSpec((1, 2 * N2, D), lambda k: (k, 0, 0)),
        out_shape=jax.ShapeDtypeStruct((n1, 2 * N2, D), F32),
        compiler_params=_cparams(("arbitrary",)),
        name="filter_stage2",
    )(a, m2)


def _spectral_kernel(a_ref, kf_ref, m_ref, mt_ref, o_ref):
    b = jnp.dot(m_ref[...], a_ref[0, 0].astype(BF16), preferred_element_type=F32)
    br, bi = b[:N2], b[N2:]
    kr, ki = kf_ref[0, :N2], kf_ref[0, N2:]
    y = jnp.concatenate([br * kr - bi * ki, br * ki + bi * kr], axis=0)
    o_ref[0, 0] = jnp.dot(mt_ref[...], y.astype(BF16), preferred_element_type=F32)


def spectral_multiply(a, kf, m2):
    s, n1 = a.shape[0], a.shape[1]
    m2b = jnp.asarray(m2).astype(BF16)
    return pl.pallas_call(
        _spectral_kernel,
        grid=(n1, s),
        in_specs=[
            pl.BlockSpec((1, 1, 2 * N2, D), lambda k, i: (i, k, 0, 0)),
            pl.BlockSpec((1, 2 * N2, D), lambda k, i: (k, 0, 0)),
            pl.BlockSpec((2 * N2, 2 * N2), lambda k, i: (0, 0)),
            pl.BlockSpec((2 * N2, 2 * N2), lambda k, i: (0, 0)),
        ],
        out_specs=pl.BlockSpec((1, 1, 2 * N2, D), lambda k, i: (i, k, 0, 0)),
        out_shape=jax.ShapeDtypeStruct(a.shape, F32),
        compiler_params=_cparams(("arbitrary", "arbitrary")),
        name="spectral_multiply",
    )(a, kf, m2b, m2b.T)


def _idft1_kernel(z_ref, g_ref, w_ref, x0_ref, skip_ref, o_ref):
    skip = skip_ref[...]
    for j in range(8):
        zj = jnp.concatenate([z_ref[0, :, 0, j, :], z_ref[0, :, 1, j, :]], axis=0).astype(BF16)
        y = jnp.dot(g_ref[j], zj, preferred_element_type=F32)
        o_ref[0, :, j, :] = (y + w_ref[0, :, j, :] * skip) * x0_ref[0, :, j, :]


def idft_stage1_gate(z5, gi, w4, x04, skip):
    s, n1 = z5.shape[0], z5.shape[1]
    n1h = w4.shape[1]
    xspec = pl.BlockSpec((1, n1h, 8, D), lambda i, t: (i, 0, t, 0))
    return pl.pallas_call(
        _idft1_kernel,
        grid=(s, N2 // 8),
        in_specs=[
            pl.BlockSpec((1, n1, 2, 8, D), lambda i, t: (i, 0, 0, t, 0)),
            pl.BlockSpec((8, n1h, 2 * n1), lambda i, t: (t, 0, 0)),
            xspec, xspec,
            pl.BlockSpec((1, D), lambda i, t: (0, 0)),
        ],
        out_specs=xspec,
        out_shape=jax.ShapeDtypeStruct(w4.shape, F32),
        compiler_params=_cparams(("arbitrary", "arbitrary")),
        name="idft_stage1_gate",
    )(z5, jnp.asarray(gi).astype(BF16), w4, x04, skip.reshape(1, D))


def long_conv_gate_two_stage(w, x0, k, skip, *, n_seq, seq_len):
    n1h = seq_len // N2
    n1 = 2 * n1h
    gf_data, gf_full, gi, m2 = _dft_tables(n1h)
    ka = dft_stage1(k.reshape(1, n1, N2, D), jnp.asarray(gf_full), precise=True)
    kf = filter_stage2(ka.reshape(1, n1, 2 * N2, D), jnp.asarray(m2))
    w4 = w.reshape(n_seq, n1h, N2, D)
    a = dft_stage1(w4, jnp.asarray(gf_data), precise=False)
    zf = spectral_multiply(a.reshape(n_seq, n1, 2 * N2, D), kf, m2)
    y = idft_stage1_gate(zf.reshape(n_seq, n1, 2, N2, D), gi, w4, x0.reshape(n_seq, n1h, N2, D), skip)
    return y.reshape(n_seq * seq_len, D)


def _filter_dft_kernel(k_ref, f_ref, o_ref):
    o_ref[...] = jnp.dot(f_ref[...], k_ref[...], precision=HIGHEST, preferred_element_type=F32)


def _direct_conv_kernel(w_ref, x0_ref, kf_ref, ff_ref, fi_ref, skip_ref, o_ref):
    n = kf_ref.shape[0] // 2
    w = w_ref[...]
    b = jnp.dot(ff_ref[...], w.astype(BF16), preferred_element_type=F32)
    br, bi = b[:n], b[n:]
    kr, ki = kf_ref[:n], kf_ref[n:]
    y = jnp.concatenate([br * kr - bi * ki, br * ki + bi * kr], axis=0)
    conv = jnp.dot(fi_ref[...], y.astype(BF16), preferred_element_type=F32)
    o_ref[...] = (conv + w * skip_ref[...]) * x0_ref[...]


def long_conv_gate_direct(w, x0, k, skip, *, n_seq, seq_len):
    n = 2 * seq_len
    ang = 2.0 * np.pi * ((np.arange(n)[:, None] * np.arange(n)[None, :]) % n) / n
    f_full = np.concatenate([np.cos(ang), -np.sin(ang)], axis=0).astype(np.float32)
    f_inv = (np.concatenate([np.cos(ang), -np.sin(ang)], axis=1)[:seq_len] / n).astype(np.float32)
    kf = pl.pallas_call(
        _filter_dft_kernel,
        out_shape=jax.ShapeDtypeStruct((2 * n, D), F32),
        compiler_params=pltpu.CompilerParams(vmem_limit_bytes=VMEM_LIMIT),
        name="filter_dft_direct",
    )(k, jnp.asarray(f_full))
    full = lambda shape: pl.BlockSpec(shape, lambda i: (0, 0))
    return pl.pallas_call(
        _direct_conv_kernel,
        grid=(n_seq,),
        in_specs=[
            pl.BlockSpec((seq_len, D), lambda i: (i, 0)),
            pl.BlockSpec((seq_len, D), lambda i: (i, 0)),
            full((2 * n, D)), full((2 * n, seq_len)), full((seq_len, 2 * n)), full((1, D)),
        ],
        out_specs=pl.BlockSpec((seq_len, D), lambda i: (i, 0)),
        out_shape=jax.ShapeDtypeStruct((n_seq * seq_len, D), F32),
        compiler_params=_cparams(("arbitrary",)),
        name="direct_conv_gate",
    )(w, x0, kf, jnp.asarray(f_full[:, :seq_len]).astype(BF16), jnp.asarray(f_inv).astype(BF16),
      skip.reshape(1, D))


def hyena_filter(seq_len, f_w1, f_b1, f_w2, f_b2, f_w3, f_b3, f_freq, f_wout):
    t = jnp.linspace(0.0, 1.0, seq_len, dtype=F32)[:, None]
    bands = (HY_EMB - 1) // 2
    w = 2 * math.pi * jnp.arange(seq_len, dtype=F32)[:, None] / seq_len
    f = jnp.linspace(1e-4, bands - 1, bands, dtype=F32)[None, :]
    z = jnp.concatenate([t, jnp.cos(f * w), -jnp.sin(f * w)], axis=-1)
    a = jnp.sin(f_freq * (jnp.dot(z, f_w1, precision=HIGHEST) + f_b1))
    a = jnp.sin(f_freq * (jnp.dot(a, f_w2, precision=HIGHEST) + f_b2))
    a = jnp.sin(f_freq * (jnp.dot(a, f_w3, precision=HIGHEST) + f_b3))
    h = jnp.dot(a, f_wout, precision=HIGHEST)
    deltas = jnp.linspace(math.log(HY_FAST_DECAY) / HY_DECAY_TARGET,
                          math.log(HY_SLOW_DECAY) / HY_DECAY_TARGET, D, dtype=F32)
    h = h * jnp.exp(-t * jnp.abs(jnp.concatenate([deltas, deltas])))
    h_fwd, h_bwd = h[:, :D], h[:, D:]
    k = jnp.concatenate([h_fwd[:1] + h_bwd[:1], h_fwd[1:], jnp.zeros((1, D), F32), h_bwd[:0:-1]], axis=0)
    return k / jnp.sum(jnp.abs(k), axis=0, keepdims=True)


def hyena_mixer(x, mod3, norm1_g, p, *, with_ctx):
    z = norm_mod_matmul(x, norm1_g, mod3, 0, 1, p['w_in'].astype(BF16), p['b_in'], tn=D, out_dtype=F32)
    fargs = (p['f_w1'], p['f_b1'], p['f_w2'], p['f_b2'], p['f_w3'], p['f_b3'], p['f_freq'], p['f_wout'])
    w, x0 = hyena_short_conv(z, p['conv_w'], p['conv_b'], seq_len=L, n_seq=NB, row_block0=0)
    y = long_conv_gate_two_stage(w, x0, hyena_filter(L, *fargs), p['skip'], n_seq=NB, seq_len=L)
    if not with_ctx:
        return y
    wc, x0c = hyena_short_conv(z, p['conv_w'], p['conv_b'], seq_len=C, n_seq=NB, row_block0=TX // C)
    yc = long_conv_gate_direct(wc, x0c, hyena_filter(C, *fargs), p['skip'], n_seq=NB, seq_len=C)
    return jnp.concatenate([y, yc], axis=0)


GM_PAIR = 2 * GM_GC


def _gm_gate_kernel(u_ref, v_ref, ws_ref, bias_ref, o_ref):
    tm = u_ref.shape[0]
    lane = lax.broadcasted_iota(jnp.int32, (1, GM_PAIR), 1)
    first = lane < GM_GC
    for ch in range(tm // GM_CHUNK):
        rows = slice(ch * GM_CHUNK, (ch + 1) * GM_CHUNK)
        for gp in range(GM_GROUPS // 2):
            cols = slice(gp * GM_PAIR, (gp + 1) * GM_PAIR)
            vp = v_ref[rows, cols]
            r0 = jnp.dot(ws_ref[2 * gp], vp, preferred_element_type=F32)
            r1 = jnp.dot(ws_ref[2 * gp + 1], vp, preferred_element_type=F32)
            r = jnp.where(first, r0, r1) + bias_ref[:, cols]
            o_ref[rows, cols] = (u_ref[rows, cols].astype(F32) * r).astype(o_ref.dtype)


def gm_spatial_gate(zu, w_s, b_s, *, rows, tm=TM):
    bias = jnp.repeat(b_s.T, GM_GC, axis=1)
    return pl.pallas_call(
        _gm_gate_kernel,
        grid=(rows // tm,),
        in_specs=[
            pl.BlockSpec((tm, GM_WIDTH), lambda i: (i, 0)),
            pl.BlockSpec((tm, GM_WIDTH), lambda i: (i, 1)),
            pl.BlockSpec((GM_GROUPS, GM_CHUNK, GM_CHUNK), lambda i: (0, 0, 0)),
            pl.BlockSpec((GM_CHUNK, GM_WIDTH), lambda i: (0, 0)),
        ],
        out_specs=pl.BlockSpec((tm, GM_WIDTH), lambda i: (i, 0)),
        out_shape=jax.ShapeDtypeStruct((zu.shape[0], GM_WIDTH), BF16),
        compiler_params=_cparams(("arbitrary",)),
        name="gm_spatial_gate",
    )(zu, zu, w_s.astype(BF16), bias)


def _final_norm_kernel(x_ref, g_ref, o_ref):
    x = x_ref[...]
    ms = jnp.mean(x * x, axis=-1, keepdims=True)
    o_ref[...] = x * lax.rsqrt(ms + EPS) * g_ref[...]


def final_norm(x, g, *, rows, tm=TM):
    return pl.pallas_call(
        _final_norm_kernel,
        grid=(rows // tm,),
        in_specs=[pl.BlockSpec((tm, D), lambda i: (i, 0)), pl.BlockSpec((1, D), lambda i: (0, 0))],
        out_specs=pl.BlockSpec((tm, D), lambda i: (i, 0)),
        out_shape=jax.ShapeDtypeStruct((rows, D), F32),
        compiler_params=_cparams(("arbitrary",)),
        name="final_norm",
    )(x, g.reshape(1, D))


def kernel(x, c, ctx, c_ctx, ada_w, ada_b, norm1_g, norm2_g, final_g, na_w_qkv, na_rpb, na_w_o, hy_w_in, hy_b_in, hy_conv_w, hy_conv_b, hy_f_w1, hy_f_b1, hy_f_w2, hy_f_b2, hy_f_w3, hy_f_b3, hy_f_freq, hy_f_wout, hy_skip, hy_w_out, hy_b_out, gm_w_in, gm_b_in, gm_ln_g, gm_ln_b, gm_w_s, gm_b_s, gm_w_out, gm_b_out, moe_router_w, moe_router_b, moe_w_gu, moe_b_gu, moe_w_dn, moe_b_dn):
    assert x.shape == (NB, L, D) and ctx.shape == (NB, C, D)
    xs = jnp.concatenate([x.reshape(TX, D), ctx.reshape(TCX, D)], axis=0)
    c_all = jnp.concatenate([c, c_ctx[None], jnp.zeros((MOD_ROWS - NB - 1, D), F32)], axis=0)
    mod = ada_modulation(c_all, ada_w, ada_b).reshape(DEPTH, MOD_ROWS * 6, 1, D)
    zero_b = jnp.zeros((D,), F32)

    for i in range(DEPTH):
        kind, j = i % 3, i // 3
        ctx_out = i < DEPTH - 1
        rows = T if ctx_out else TX
        mod3 = mod[i]
        if kind == 0:
            qkv = norm_mod_matmul(xs, norm1_g[i], mod3, 0, 1, na_w_qkv[j].astype(BF16), jnp.zeros((3 * D,), F32),
                                  tn=D, out_dtype=BF16)
            o, oc = neighbourhood_attention(qkv, _na_bias_tables(na_rpb[j]), need_ctx=ctx_out)
            y = jnp.concatenate([o, oc], axis=0) if ctx_out else o
            xs = matmul_gated_residual(y, na_w_o[j].astype(BF16), zero_b, xs, mod3, 2, rows=rows)
        elif kind == 1:
            p = dict(w_in=hy_w_in[j], b_in=hy_b_in[j], conv_w=hy_conv_w[j], conv_b=hy_conv_b[j],
                     f_w1=hy_f_w1[j], f_b1=hy_f_b1[j], f_w2=hy_f_w2[j], f_b2=hy_f_b2[j], f_w3=hy_f_w3[j],
                     f_b3=hy_f_b3[j], f_freq=hy_f_freq[j], f_wout=hy_f_wout[j], skip=hy_skip[j])
            y = hyena_mixer(xs, mod3, norm1_g[i], p, with_ctx=ctx_out)
            xs = matmul_gated_residual(y, hy_w_out[j].astype(BF16), hy_b_out[j], xs, mod3, 2, rows=rows)
        else:
            zu = norm_mod_matmul(xs, norm1_g[i], mod3, 0, 1, gm_w_in[j].astype(BF16), gm_b_in[j],
                                 tn=GM_WIDTH, out_dtype=BF16, tm=TM_GM, ln=(gm_ln_g[j], gm_ln_b[j]))
            y = gm_spatial_gate(zu, gm_w_s[j], gm_b_s[j], rows=rows)
            xs = matmul_gated_residual(y, gm_w_out[j].astype(BF16), gm_b_out[j], xs, mod3, 2, rows=rows)
        xs = moe_layer(xs, mod3, norm2_g[i], moe_router_w[i], moe_router_b[i], moe_w_gu[i], moe_b_gu[i],
                       moe_w_dn[i], moe_b_dn[i], rows=rows)
    return final_norm(xs, final_g, rows=TX).reshape(NB, L, D)
```
